```python
import math, functools
import jax, jax.numpy as jnp
from jax import lax
import numpy as np

D_MODEL = 1024
BATCH = 8
SEQ = 2048
DEPTH = 4
DEC_BATCH = 32
DEC_SEQ = 1
PAST_LEN = 8192
PAGE_SIZE = 128

HEAD_DIM = 64
N_SB_HEADS = 8
N_DIFF_HEADS = 4
N_QK_HEADS = N_SB_HEADS + 2 * N_DIFF_HEADS
ATT_WIDTH = N_QK_HEADS * HEAD_DIM
QBLOCK = 128
C_DIM = D_MODEL // 2
C_GROUPS = 4
CHUNK = 128
D_CH = D_MODEL // 2
D_CONV = 31
FFN_DIM = 2816
FFN_CONV = 3
N_ATT_LAYERS = (DEPTH + 1) // 2
N_MIX_LAYERS = DEPTH // 2
N_MOD = 6
EPS = 1e-6
NEG_INF = -1e30

kernel_name = "hybrid_stickbreak_diffattn_gmlp_conformer_convffn_step"


def rms_norm(x, g):
    xf = x.astype(jnp.float32)
    y = xf * lax.rsqrt(jnp.mean(xf * xf, axis=-1, keepdims=True) + EPS)
    return (y * g.astype(jnp.float32)).astype(x.dtype)


def layer_norm(x, g, b):
    xf = x.astype(jnp.float32)
    xc = xf - jnp.mean(xf, axis=-1, keepdims=True)
    var = jnp.mean(xc * xc, axis=-1, keepdims=True)
    return (xc * lax.rsqrt(var + EPS) * g.astype(jnp.float32) + b.astype(jnp.float32)).astype(x.dtype)


def causal_dwconv(x, past, w, b):
    k = w.shape[0]
    xp = jnp.concatenate([past.astype(x.dtype), x], axis=1)
    y = lax.conv_general_dilated(xp, w[:, None, :].astype(x.dtype), (1,), 'VALID',
                                 dimension_numbers=('NWC', 'WIO', 'NWC'),
                                 feature_group_count=x.shape[-1])
    return y + b.astype(x.dtype), xp[:, xp.shape[1] - (k - 1):]


def gather_pages(pool, page_table):
    g = pool[page_table]
    return g.reshape((g.shape[0], g.shape[1] * g.shape[2]) + pool.shape[2:])


def attn_query_block(q, q_pos, k, v, k_pos, lam, sub_g, out_scale):
    B, Tq = q.shape[0], q.shape[1]
    Tk = k.shape[1]
    scale = HEAD_DIM ** -0.5
    z = jnp.einsum('bqhd,bkhd->bhqk', q[:, :, :N_SB_HEADS], k[:, :, :N_SB_HEADS]).astype(jnp.float32) * scale
    strict = k_pos[None, :] < q_pos[:, None]
    log_keep = jnp.where(strict, jax.nn.log_sigmoid(-z), 0.0)
    suffix = lax.cumsum(log_keep, axis=3, reverse=True)
    suffix_excl = jnp.concatenate([suffix[..., 1:], jnp.zeros_like(suffix[..., :1])], axis=-1)
    a_sb = jnp.where(strict, jnp.exp(jax.nn.log_sigmoid(z) + suffix_excl), 0.0)
    o_sb = jnp.einsum('bhqk,bkhd->bqhd', a_sb, v[:, :, :N_SB_HEADS].astype(jnp.float32))
    qd = q[:, :, N_SB_HEADS:].reshape(B, Tq, N_DIFF_HEADS, 2, HEAD_DIM)
    kd = k[:, :, N_SB_HEADS:].reshape(B, Tk, N_DIFF_HEADS, 2, HEAD_DIM)
    vd = v[:, :, N_SB_HEADS:].reshape(B, Tk, N_DIFF_HEADS, 2 * HEAD_DIM)
    s = jnp.einsum('bqhcd,bkhcd->bhcqk', qd, kd).astype(jnp.float32) * scale
    causal = k_pos[None, :] <= q_pos[:, None]
    p = jax.nn.softmax(jnp.where(causal, s, NEG_INF), axis=-1)
    a_d = p[:, :, 0] - lam * p[:, :, 1]
    o_d = jnp.einsum('bhqk,bkhe->bqhe', a_d, vd.astype(jnp.float32))
    o_d = rms_norm(o_d, sub_g) * out_scale
    return jnp.concatenate([o_sb.reshape(B, Tq, N_SB_HEADS * HEAD_DIM),
                            o_d.reshape(B, Tq, N_DIFF_HEADS * 2 * HEAD_DIM)], axis=-1).astype(q.dtype)


def attn_mixer(h, w_in, w_out, lam_p, sub_g, lam_init, k_past, v_past):
    B, T, _ = h.shape
    qkv = jnp.einsum('btd,dn->btn', h, w_in).reshape(B, T, 3, N_QK_HEADS, HEAD_DIM)
    q, k, v = qkv[:, :, 0], qkv[:, :, 1], qkv[:, :, 2]
    if k_past is None:
        k_all, v_all, pos0 = k, v, 0
    else:
        k_all = jnp.concatenate([k_past.astype(k.dtype), k], axis=1)
        v_all = jnp.concatenate([v_past.astype(v.dtype), v], axis=1)
        pos0 = k_past.shape[1]
    k_pos = jnp.arange(k_all.shape[1], dtype=jnp.int32)
    q_pos = pos0 + jnp.arange(T, dtype=jnp.int32)
    lp = lam_p.astype(jnp.float32)
    lam = jnp.exp(jnp.sum(lp[0] * lp[1])) - jnp.exp(jnp.sum(lp[2] * lp[3])) + lam_init
    block = functools.partial(attn_query_block, k=k_all, v=v_all, k_pos=k_pos, lam=lam,
                              sub_g=sub_g, out_scale=1.0 - lam_init)
    if T > QBLOCK and T % QBLOCK == 0:
        nb = T // QBLOCK
        qb = q.reshape(B, nb, QBLOCK, N_QK_HEADS, HEAD_DIM).swapaxes(0, 1)
        pb = q_pos.reshape(nb, QBLOCK)
        o = lax.map(lambda a: block(a[0], a[1]), (qb, pb))
        o = o.swapaxes(0, 1).reshape(B, T, ATT_WIDTH)
    else:
        o = block(q, q_pos)
    return jnp.einsum('btn,nd->btd', o, w_out), k, v


def chunk_conv_mixer(h, w_in, w_out, c_ln_g, c_ln_b, ws, bs, dw, db, d_ln_g, d_ln_b, conv_past):
    B, T, _ = h.shape
    z = jnp.einsum('btd,dn->btn', h, w_in)
    uv = jax.nn.gelu(z[..., :2 * C_DIM])
    u = uv[..., :C_DIM]
    v = layer_norm(uv[..., C_DIM:], c_ln_g, c_ln_b)
    L = min(T, CHUNK)
    n = T // L
    vc = v.reshape(B, n, L, C_GROUPS, C_DIM // C_GROUPS)
    w_mask = jnp.tril(ws[:, :L, :L])
    mixed = jnp.einsum('gts,bnsgc->bntgc', w_mask, vc) + bs[:, :L].T[:, :, None]
    s_out = u * mixed.reshape(B, T, C_DIM)
    a = z[..., 2 * C_DIM:3 * C_DIM] * jax.nn.sigmoid(z[..., 3 * C_DIM:])
    if conv_past is None:
        conv_past = jnp.zeros((B, D_CONV - 1, D_CH), a.dtype)
    dc, conv_new = causal_dwconv(a, conv_past, dw, db)
    d_out = jax.nn.silu(layer_norm(dc, d_ln_g, d_ln_b))
    out = jnp.einsum('btn,nd->btd', jnp.concatenate([s_out, d_out], axis=-1), w_out)
    return out, v[:, T - L:], conv_new


def conv_ffn(h, w_in, cw, cb, w_out, past):
    B = h.shape[0]
    gu = jnp.einsum('btd,df->btf', h, w_in)
    g, u = gu[..., :FFN_DIM], gu[..., FFN_DIM:]
    if past is None:
        past = jnp.zeros((B, FFN_CONV - 1, FFN_DIM), g.dtype)
    gc, new = causal_dwconv(g, past, cw, cb)
    return jnp.einsum('btf,fd->btd', jax.nn.silu(gc) * u, w_out), new


def run_trunk(x, c, p, paged, convd_past, ffn_past):
    new_k, new_v, new_cv, new_cd, new_ff = [], [], [], [], []
    B = c.shape[0]
    for i in range(DEPTH):
        j = i // 2
        mod = (jnp.einsum('bd,dn->bn', jax.nn.silu(c), p['ada_w'][i]) + p['ada_b'][i]).reshape(B, N_MOD, 1, D_MODEL)
        sh_m, sc_m, g_m, sh_f, sc_f, g_f = [mod[:, m] for m in range(N_MOD)]
        h = rms_norm(x, p['norm_g'][i, 0]) * (1 + sc_m) + sh_m
        if i % 2 == 0:
            if paged is None:
                kp, vp = None, None
            else:
                kp = gather_pages(paged[0][j], paged[2])
                vp = gather_pages(paged[1][j], paged[2])
            o, k_new, v_new = attn_mixer(h, p['att_w_in'][j], p['att_w_out'][j], p['att_lam'][j],
                                         p['att_sub_g'][j], 0.8 - 0.6 * math.exp(-0.3 * i), kp, vp)
            new_k.append(k_new)
            new_v.append(v_new)
        else:
            o, cv, cd = chunk_conv_mixer(h, p['mix_w_in'][j], p['mix_w_out'][j], p['c_ln_g'][j], p['c_ln_b'][j],
                                         p['c_ws'][j], p['c_bs'][j], p['d_conv_w'][j], p['d_conv_b'][j],
                                         p['d_ln_g'][j], p['d_ln_b'][j],
                                         None if convd_past is None else convd_past[j])
            new_cv.append(cv)
            new_cd.append(cd)
        x = x + (g_m * o).astype(x.dtype)
        h = rms_norm(x, p['norm_g'][i, 1]) * (1 + sc_f) + sh_f
        o, ff = conv_ffn(h, p['ffn_w_in'][i], p['ffn_conv_w'][i], p['ffn_conv_b'][i], p['ffn_w_out'][i],
                         None if ffn_past is None else ffn_past[i])
        new_ff.append(ff)
        x = x + (g_f * o).astype(x.dtype)
    y = rms_norm(x, p['final_g'])
    return y, jnp.stack(new_k), jnp.stack(new_v), jnp.stack(new_cv), jnp.stack(new_cd), jnp.stack(new_ff)


def setup_inputs(seed: int = 0) -> dict:
    key = jax.random.key(seed)
    ks = jax.random.split(key, 40)
    f32 = jnp.float32

    def nrm(k, shape, s):
        return jax.random.normal(k, shape, f32) * s

    n_pages = PAST_LEN // PAGE_SIZE
    n_used = DEC_BATCH * n_pages
    n_phys = n_used + max(1, n_used // 4)
    page_table = jax.random.permutation(ks[8], n_phys)[:n_used].reshape(DEC_BATCH, n_pages).astype(jnp.int32)
    return {
        "x_prompt": nrm(ks[0], (BATCH, SEQ, D_MODEL), 1.0),
        "x_sample": nrm(ks[1], (DEC_BATCH, DEC_SEQ, D_MODEL), 1.0),
        "c_prompt": nrm(ks[2], (BATCH, D_MODEL), 1.0),
        "c_sample": nrm(ks[3], (DEC_BATCH, D_MODEL), 1.0),
        "cache_k": nrm(ks[4], (N_ATT_LAYERS, n_phys, PAGE_SIZE, N_QK_HEADS, HEAD_DIM), 1.0),
        "cache_v": nrm(ks[5], (N_ATT_LAYERS, n_phys, PAGE_SIZE, N_QK_HEADS, HEAD_DIM), 1.0),
        "state_conv_d": nrm(ks[6], (N_MIX_LAYERS, DEC_BATCH, D_CONV - 1, D_CH), 0.5),
        "state_ffn_conv": nrm(ks[7], (DEPTH, DEC_BATCH, FFN_CONV - 1, FFN_DIM), 1.0),
        "page_table": page_table,
        "norm_g": 1.0 + nrm(ks[9], (DEPTH, 2, D_MODEL), 0.02),
        "ada_w": nrm(ks[10], (DEPTH, D_MODEL, N_MOD * D_MODEL), 0.5 * D_MODEL ** -0.5),
        "ada_b": nrm(ks[11], (DEPTH, N_MOD * D_MODEL), 0.02),
        "att_w_in": nrm(ks[12], (N_ATT_LAYERS, D_MODEL, 3 * ATT_WIDTH), D_MODEL ** -0.5),
        "att_lam": nrm(ks[13], (N_ATT_LAYERS, 4, HEAD_DIM), 0.1),
        "att_sub_g": 1.0 + nrm(ks[14], (N_ATT_LAYERS, 2 * HEAD_DIM), 0.02),
        "att_w_out": nrm(ks[15], (N_ATT_LAYERS, ATT_WIDTH, D_MODEL), ATT_WIDTH ** -0.5),
        "mix_w_in": nrm(ks[16], (N_MIX_LAYERS, D_MODEL, 4 * C_DIM), D_MODEL ** -0.5),
        "c_ln_g": 1.0 + nrm(ks[17], (N_MIX_LAYERS, C_DIM), 0.02),
        "c_ln_b": nrm(ks[18], (N_MIX_LAYERS, C_DIM), 0.02),
        "c_ws": nrm(ks[19], (N_MIX_LAYERS, C_GROUPS, CHUNK, CHUNK), CHUNK ** -0.5),
        "c_bs": 1.0 + nrm(ks[20], (N_MIX_LAYERS, C_GROUPS, CHUNK), 0.02),
        "d_conv_w": nrm(ks[21], (N_MIX_LAYERS, D_CONV, D_CH), D_CONV ** -0.5),
        "d_conv_b": nrm(ks[22], (N_MIX_LAYERS, D_CH), 0.02),
        "d_ln_g": 1.0 + nrm(ks[23], (N_MIX_LAYERS, D_CH), 0.02),
        "d_ln_b": nrm(ks[24], (N_MIX_LAYERS, D_CH), 0.02),
        "mix_w_out": nrm(ks[25], (N_MIX_LAYERS, C_DIM + D_CH, D_MODEL), (C_DIM + D_CH) ** -0.5),
        "ffn_w_in": nrm(ks[26], (DEPTH, D_MODEL, 2 * FFN_DIM), D_MODEL ** -0.5),
        "ffn_conv_w": nrm(ks[27], (DEPTH, FFN_CONV, FFN_DIM), FFN_CONV ** -0.5),
        "ffn_conv_b": nrm(ks[28], (DEPTH, FFN_DIM), 0.02),
        "ffn_w_out": nrm(ks[29], (DEPTH, FFN_DIM, D_MODEL), FFN_DIM ** -0.5),
        "final_g": 1.0 + nrm(ks[30], (D_MODEL,), 0.02),
    }


def reference(x_prompt, x_sample, c_prompt, c_sample, cache_k, cache_v, state_conv_d, state_ffn_conv,
              page_table, norm_g, ada_w, ada_b, att_w_in, att_lam, att_sub_g, att_w_out, mix_w_in,
              c_ln_g, c_ln_b, c_ws, c_bs, d_conv_w, d_conv_b, d_ln_g, d_ln_b, mix_w_out,
              ffn_w_in, ffn_conv_w, ffn_conv_b, ffn_w_out, final_g):
    p = dict(norm_g=norm_g, ada_w=ada_w, ada_b=ada_b, att_w_in=att_w_in, att_lam=att_lam,
             att_sub_g=att_sub_g, att_w_out=att_w_out, mix_w_in=mix_w_in, c_ln_g=c_ln_g, c_ln_b=c_ln_b,
             c_ws=c_ws, c_bs=c_bs, d_conv_w=d_conv_w, d_conv_b=d_conv_b, d_ln_g=d_ln_g, d_ln_b=d_ln_b,
             mix_w_out=mix_w_out, ffn_w_in=ffn_w_in, ffn_conv_w=ffn_conv_w, ffn_conv_b=ffn_conv_b,
             ffn_w_out=ffn_w_out, final_g=final_g)
    y_prompt, k_p, v_p, cv_p, cd_p, ff_p = run_trunk(x_prompt, c_prompt, p, None, None, None)
    y_sample, k_s, v_s, cv_s, cd_s, ff_s = run_trunk(x_sample, c_sample, p, (cache_k, cache_v, page_table),
                                                     state_conv_d, state_ffn_conv)
    return (y_prompt, y_sample, k_p, v_p, k_s, v_s, cv_p, cv_s, cd_p, cd_s, ff_p, ff_s)
```

```python
import functools
import math

import jax
import jax.numpy as jnp
from jax import lax
from jax.experimental import pallas as pl
from jax.experimental.pallas import tpu as pltpu

F32 = jnp.float32
BF16 = jnp.bfloat16

HEAD_DIM = 64
N_SB_HEADS = 8
N_DIFF_HEADS = 4
N_MOD = 6
C_GROUPS = 4
CHUNK = 128
D_CONV = 31
FFN_CONV = 3
EPS = 1e-6
NEG_INF = -1e30

LANES = 128
SUBLANES = 8
VMEM_LIMIT = 56 * 1024 * 1024

TM_PROJ = 512
TM_FFN = 256
TM_MIX = 256
TQ = 256
PAGES_PER_STEP = 4
FFN_CHUNK = 256
FFN_CHUNK_DEC = 1408


def _cparams(sem, vmem=VMEM_LIMIT):
    return pltpu.CompilerParams(dimension_semantics=sem, vmem_limit_bytes=vmem)


def _resident(shape):
    nd = len(shape)
    return pl.BlockSpec(shape, lambda *_: (0,) * nd, pipeline_mode=pl.Buffered(1))


def _norm_mod(x, g, sc, sh):
    ms = jnp.mean(x * x, axis=-1, keepdims=True)
    y = x * lax.rsqrt(ms + EPS) * g
    return y * (1.0 + sc) + sh


def _mod_row(mod_ref, slot, seq, per_row):
    if per_row:
        return mod_ref[slot]
    return mod_ref[slot, pl.ds(seq, 1), :]


def _layer_norm(x, g, b):
    mu = jnp.mean(x, axis=-1, keepdims=True)
    xc = x - mu
    var = jnp.mean(xc * xc, axis=-1, keepdims=True)
    return xc * lax.rsqrt(var + EPS) * g + b


def _gelu_tanh(x):
    c = math.sqrt(2.0 / math.pi)
    return x * (0.5 * (1.0 + jnp.tanh(c * (x + 0.044715 * (x * x * x)))))


def _silu(x):
    return x * jax.nn.sigmoid(x)


def _softplus(z):
    return jnp.maximum(z, 0.0) + jnp.log1p(jnp.exp(-jnp.abs(z)))


def _split_bf16(x):
    hi = x.astype(BF16)
    lo = (x - hi.astype(F32)).astype(BF16)
    return hi, lo


def _ada_kernel(c_ref, w_ref, b_ref, o_ref):
    c = c_ref[...]
    s = _silu(c).astype(BF16)
    w = w_ref[...].astype(BF16)
    o_ref[...] = jnp.dot(s, w, preferred_element_type=F32) + b_ref[...]


def _ada_mod(c_all, ada_w, ada_b):
    depth, d, _ = ada_w.shape
    rows = c_all.shape[0]
    return pl.pallas_call(
        _ada_kernel,
        out_shape=jax.ShapeDtypeStruct((depth, N_MOD, rows, d), F32),
        grid=(depth, N_MOD),
        in_specs=[
            pl.BlockSpec((rows, d), lambda i, j: (0, 0)),
            pl.BlockSpec((None, d, d), lambda i, j: (i, 0, j)),
            pl.BlockSpec((None, None, 1, d), lambda i, j: (i, j, 0, 0)),
        ],
        out_specs=pl.BlockSpec((None, None, rows, d), lambda i, j: (i, j, 0, 0)),
        compiler_params=_cparams(("parallel", "parallel")),
        name="ada_mod",
    )(c_all, ada_w, ada_b.reshape(depth, N_MOD, 1, d))


def _qkv_kernel(x_ref, g_ref, mod_ref, w_ref, q_ref, k_ref, v_ref, kb_ref, vb_ref,
                *, tiles_per_seq, per_row, width):
    seq = pl.program_id(0) // tiles_per_seq
    sh = _mod_row(mod_ref, 0, seq, per_row)
    sc = _mod_row(mod_ref, 1, seq, per_row)
    h = _norm_mod(x_ref[...], g_ref[...], sc, sh).astype(BF16)
    q = jnp.dot(h, w_ref[:, 0:width], preferred_element_type=F32)
    q_ref[...] = (q * (HEAD_DIM ** -0.5)).astype(BF16)
    k = jnp.dot(h, w_ref[:, width:2 * width], preferred_element_type=F32)
    k_ref[...] = k
    kb_ref[...] = k.astype(BF16)
    v = jnp.dot(h, w_ref[:, 2 * width:3 * width], preferred_element_type=F32)
    v_ref[...] = v
    vb_ref[...] = v.astype(BF16)


def _qkv_proj(x, norm_g, mod, w_bf, *, tm, tiles_per_seq, per_row):
    m, d = x.shape
    width = w_bf.shape[1] // 3
    tile = lambda dt: jax.ShapeDtypeStruct((m, width), dt)
    ospec = pl.BlockSpec((tm, width), lambda i: (i, 0))
    return pl.pallas_call(
        functools.partial(_qkv_kernel, tiles_per_seq=tiles_per_seq, per_row=per_row, width=width),
        out_shape=(tile(BF16), tile(F32), tile(F32), tile(BF16), tile(BF16)),
        grid=(m // tm,),
        in_specs=[
            pl.BlockSpec((tm, d), lambda i: (i, 0)),
            _resident((1, d)),
            _resident(mod.shape),
            _resident(w_bf.shape),
        ],
        out_specs=(ospec,) * 5,
        compiler_params=_cparams(("parallel",)),
        name="qkv_proj",
    )(x, norm_g.reshape(1, d), mod, w_bf)


def _oproj_kernel(x_ref, oa_ref, ob_ref, mod_ref, w_ref, y_ref, *, tiles_per_seq, per_row, split):
    seq = pl.program_id(0) // tiles_per_seq
    gate = _mod_row(mod_ref, 2, seq, per_row)
    o = jnp.dot(oa_ref[...], w_ref[0:split, :], preferred_element_type=F32)
    o = o + jnp.dot(ob_ref[...], w_ref[split:, :], preferred_element_type=F32)
    y_ref[...] = x_ref[...] + gate * o


def _out_proj(x, oa, ob, mod, w_bf, *, tm, tiles_per_seq, per_row):
    m, d = x.shape
    split = oa.shape[1]
    return pl.pallas_call(
        functools.partial(_oproj_kernel, tiles_per_seq=tiles_per_seq, per_row=per_row, split=split),
        out_shape=jax.ShapeDtypeStruct((m, d), F32),
        grid=(m // tm,),
        in_specs=[
            pl.BlockSpec((tm, d), lambda i: (i, 0)),
            pl.BlockSpec((tm, split), lambda i: (i, 0)),
            pl.BlockSpec((tm, ob.shape[1]), lambda i: (i, 0)),
            _resident(mod.shape),
            _resident(w_bf.shape),
        ],
        out_specs=pl.BlockSpec((tm, d), lambda i: (i, 0)),
        compiler_params=_cparams(("parallel",)),
        name="out_proj",
    )(x, oa, ob, mod, w_bf)


def _sb_kernel(q_ref, k_ref, v_ref, uu_ref, o_ref, *, tq):
    i = pl.program_id(2)
    q = q_ref[...]
    lane = lax.broadcasted_iota(jnp.int32, (1, LANES), 1)
    row = lax.broadcasted_iota(jnp.int32, (tq, tq), 0)
    col = lax.broadcasted_iota(jnp.int32, (tq, tq), 1)
    strict = col < row
    uu = uu_ref[...]
    zero_q = jnp.zeros_like(q)
    outs = []
    for half in range(2):
        qh = jnp.where((lane < HEAD_DIM) if half == 0 else (lane >= HEAD_DIM), q, zero_q)

        def block(j, carry, acc, diag, qh=qh):
            start = pl.multiple_of(j * tq, tq)
            kb = k_ref[pl.ds(start, tq), :]
            vb = v_ref[pl.ds(start, tq), :]
            z = lax.dot_general(qh, kb, (((1,), (1,)), ((), ())), preferred_element_type=F32)
            sp = _softplus(z)
            lk = jnp.where(strict, sp, 0.0) if diag else sp
            hi, lo = _split_bf16(lk)
            cs = jnp.dot(jnp.concatenate([hi, lo], axis=1), uu, preferred_element_type=F32)
            a = jnp.exp(z - sp - cs - carry)
            if diag:
                a = jnp.where(strict, a, 0.0)
            acc = acc + jnp.dot(a.astype(BF16), vb, preferred_element_type=F32)
            carry = carry + jnp.sum(lk, axis=1, keepdims=True)
            return carry, acc

        carry, acc = block(i, jnp.zeros((tq, 1), F32), jnp.zeros((tq, LANES), F32), True)

        def body(t, c, block=block):
            return block(i - 1 - t, c[0], c[1], False)

        carry, acc = lax.fori_loop(0, i, body, (carry, acc))
        outs.append(acc)
    o_ref[...] = jnp.where(lane < HEAD_DIM, outs[0], outs[1]).astype(o_ref.dtype)


def _suffix_matrix(n):
    j = lax.broadcasted_iota(jnp.int32, (n, n), 0)
    s = lax.broadcasted_iota(jnp.int32, (n, n), 1)
    u = (j > s).astype(BF16)
    return jnp.concatenate([u, u], axis=0)


def _attn_sb(q_bf, kb, vb, *, batch, seq_len, tq):
    m = q_bf.shape[0]
    nq = seq_len // tq
    n_pairs = N_SB_HEADS * HEAD_DIM // LANES
    return pl.pallas_call(
        functools.partial(_sb_kernel, tq=tq),
        out_shape=jax.ShapeDtypeStruct((m, n_pairs * LANES), BF16),
        grid=(batch, n_pairs, nq),
        in_specs=[
            pl.BlockSpec((tq, LANES), lambda b, p, i: (b * nq + i, p)),
            pl.BlockSpec((seq_len, LANES), lambda b, p, i: (b, p)),
            pl.BlockSpec((seq_len, LANES), lambda b, p, i: (b, p)),
            _resident((2 * tq, tq)),
        ],
        out_specs=pl.BlockSpec((tq, LANES), lambda b, p, i: (b * nq + i, p)),
        compiler_params=_cparams(("parallel", "parallel", "parallel")),
        name="attn_sb",
    )(q_bf, kb, vb, _suffix_matrix(tq))


def _lam_value(lam_ref, lam_init):
    lp = lam_ref[...]
    t1 = jnp.sum(lp[0:1] * lp[1:2], axis=1, keepdims=True)
    t2 = jnp.sum(lp[2:3] * lp[3:4], axis=1, keepdims=True)
    return jnp.exp(t1) - jnp.exp(t2) + lam_init


def _diff_kernel(lam_ref, g_ref, q_ref, k_ref, v_ref, o_ref, vext_ref, *, tq, lam_init):
    i = pl.program_id(2)

    @pl.when(i == 0)
    def _():
        vext_ref[:, 0:LANES] = v_ref[...]
        vext_ref[:, LANES:2 * LANES] = jnp.ones((v_ref.shape[0], LANES), BF16)

    q = q_ref[...]
    lane = lax.broadcasted_iota(jnp.int32, (1, LANES), 1)
    row = lax.broadcasted_iota(jnp.int32, (tq, tq), 0)
    col = lax.broadcasted_iota(jnp.int32, (tq, tq), 1)
    causal = col <= row
    zero_q = jnp.zeros_like(q)
    qs = (jnp.where(lane < HEAD_DIM, q, zero_q), jnp.where(lane >= HEAD_DIM, q, zero_q))

    def block(j, state, diag):
        start = pl.multiple_of(j * tq, tq)
        kb = k_ref[pl.ds(start, tq), :]
        ve = vext_ref[pl.ds(start, tq), :]
        new = []
        for c in range(2):
            m_old, acc = state[2 * c], state[2 * c + 1]
            s = lax.dot_general(qs[c], kb, (((1,), (1,)), ((), ())), preferred_element_type=F32)
            if diag:
                s = jnp.where(causal, s, NEG_INF)
            m_new = jnp.maximum(m_old, jnp.max(s, axis=1, keepdims=True))
            p = jnp.exp(s - m_new)
            alpha = jnp.exp(m_old - m_new)
            acc = alpha * acc + jnp.dot(p.astype(BF16), ve, preferred_element_type=F32)
            new += [m_new, acc]
        return tuple(new)

    m0 = jnp.full((tq, 1), NEG_INF, F32)
    a0 = jnp.zeros((tq, 2 * LANES), F32)
    state = block(i, (m0, a0, m0, a0), True)
    state = lax.fori_loop(0, i, lambda t, st: block(i - 1 - t, st, False), state)

    lam = _lam_value(lam_ref, lam_init)
    o1 = state[1][:, 0:LANES] / state[1][:, LANES:2 * LANES]
    o2 = state[3][:, 0:LANES] / state[3][:, LANES:2 * LANES]
    o = o1 - lam * o2
    ms = jnp.mean(o * o, axis=1, keepdims=True)
    o = o * lax.rsqrt(ms + EPS) * g_ref[...] * (1.0 - lam_init)
    o_ref[...] = o.astype(o_ref.dtype)


def _attn_diff(q_bf, kb, vb, lam_p, sub_g, *, batch, seq_len, tq, lam_init):
    m = q_bf.shape[0]
    nq = seq_len // tq
    off = N_SB_HEADS * HEAD_DIM // LANES
    return pl.pallas_call(
        functools.partial(_diff_kernel, tq=tq, lam_init=lam_init),
        out_shape=jax.ShapeDtypeStruct((m, N_DIFF_HEADS * LANES), BF16),
        grid=(batch, N_DIFF_HEADS, nq),
        in_specs=[
            _resident(lam_p.shape),
            _resident((1, LANES)),
            pl.BlockSpec((tq, LANES), lambda b, p, i: (b * nq + i, off + p)),
            pl.BlockSpec((seq_len, LANES), lambda b, p, i: (b, off + p)),
            pl.BlockSpec((seq_len, LANES), lambda b, p, i: (b, off + p)),
        ],
        out_specs=pl.BlockSpec((tq, LANES), lambda b, p, i: (b * nq + i, p)),
        scratch_shapes=[pltpu.VMEM((seq_len, 2 * LANES), BF16)],
        compiler_params=_cparams(("parallel", "parallel", "arbitrary")),
        name="attn_diff",
    )(lam_p, sub_g.reshape(1, LANES), q_bf, kb, vb)


def _dec_attn_kernel(pt_ref, qb_ref, kx_ref, vx_ref, lam_ref, g_ref, uo_ref, msuf_ref,
                     *rest, n_steps, pages, n_heads, lam_init):
    k_refs = rest[0:pages]
    v_refs = rest[pages:2 * pages]
    o_ref = rest[2 * pages]
    s_scr, w_scr, x_scr, acc_scr = rest[2 * pages + 1:]
    del pt_ref
    t = pl.program_id(1)
    n_pages = n_steps * pages
    n_rows = n_pages * n_heads
    lane = lax.broadcasted_iota(jnp.int32, (1, LANES), 1)
    rid = lax.broadcasted_iota(jnp.int32, (n_heads, 1), 0)
    is_sb = rid < N_SB_HEADS

    def value_row(wp, h):
        if h < N_SB_HEADS:
            return wp[h:h + 1, :]
        first = N_SB_HEADS + 2 * ((h - N_SB_HEADS) // 2)
        return wp[first:first + 1, :] + wp[first + 1:first + 2, :]

    @pl.when(t == 0)
    def _():
        acc_scr[...] = jnp.zeros_like(acc_scr)

    @pl.when(t < n_steps)
    def _():
        for h in range(n_heads):
            qh = qb_ref[h]
            for r in range(pages):
                row = jnp.sum(k_refs[r][h] * qh, axis=0, keepdims=True)
                s_scr[t * pages + r, pl.ds(h, 1), :] = row

    @pl.when(t == n_steps - 1)
    def _():
        for h in range(n_heads):
            x_scr[pl.ds(h, 1), :] = jnp.sum(kx_ref[h] * qb_ref[h], axis=0, keepdims=True)
        s3 = s_scr[...]
        s = s3.reshape(n_rows, LANES)
        sp = _softplus(s)
        hi, lo = _split_bf16(sp)
        r1 = jnp.dot(jnp.concatenate([hi, lo], axis=1), uo_ref[...], preferred_element_type=F32)
        cs = r1[:, 0:LANES]
        tot = r1[:, LANES:2 * LANES]
        t1 = tot.astype(BF16)
        rem = tot - t1.astype(F32)
        t2, t3 = _split_bf16(rem)
        c3 = jnp.dot(msuf_ref[...], jnp.concatenate([t1, t2, t3], axis=1), preferred_element_type=F32)
        carry = c3[:, 0:LANES] + c3[:, LANES:2 * LANES] + c3[:, 2 * LANES:3 * LANES]
        a3 = jnp.exp(s - sp - cs - carry).reshape(n_pages, n_heads, LANES)
        sx = jnp.where(lane == 0, x_scr[...], NEG_INF)
        m = jnp.max(jnp.maximum(jnp.max(s3, axis=0), sx), axis=1, keepdims=True)
        p3 = jnp.exp(s3 - m[None])
        px = jnp.exp(sx - m)
        l = jnp.sum(jnp.sum(p3, axis=0) + px, axis=1, keepdims=True)
        lam = _lam_value(lam_ref, lam_init)
        coef = jnp.where(lax.bitwise_and(rid, 1) == 0, 1.0, -lam) / l
        w_scr[...] = jnp.where(is_sb[None], a3, p3 * coef[None])
        x_scr[...] = jnp.where(is_sb, 0.0, px * coef)

    @pl.when(t >= n_steps)
    def _():
        blk = t - n_steps
        wps = [w_scr[blk * pages + r] for r in range(pages)]
        for h in range(n_heads):
            acc = acc_scr[h]
            for r in range(pages):
                acc = acc + value_row(wps[r], h) * v_refs[r][h]
            acc_scr[h] = acc

    @pl.when(t == 2 * n_steps - 1)
    def _():
        wx = x_scr[...]
        for h in range(n_heads):
            acc_scr[h] = acc_scr[h] + value_row(wx, h) * vx_ref[h]
        hi, lo = _split_bf16(acc_scr[...].reshape(n_heads * HEAD_DIM, LANES))
        ones = jnp.ones((SUBLANES, LANES), BF16)
        nt = (((1,), (1,)), ((), ()))
        o = lax.dot_general(ones, hi, nt, preferred_element_type=F32)
        o = (o + lax.dot_general(ones, lo, nt, preferred_element_type=F32))[0:1, :]
        n_sb = N_SB_HEADS * HEAD_DIM
        o_ref[:, 0:n_sb] = o[:, 0:n_sb].astype(o_ref.dtype)
        for hd in range(N_DIFF_HEADS):
            od = o[:, n_sb + hd * LANES:n_sb + (hd + 1) * LANES]
            ms = jnp.mean(od * od, axis=1, keepdims=True)
            od = od * lax.rsqrt(ms + EPS) * g_ref[...] * (1.0 - lam_init)
            o_ref[:, n_sb + hd * LANES:n_sb + (hd + 1) * LANES] = od.astype(o_ref.dtype)


def _suffix_and_total_matrix():
    j = lax.broadcasted_iota(jnp.int32, (LANES, LANES), 0)
    s = lax.broadcasted_iota(jnp.int32, (LANES, LANES), 1)
    u1 = jnp.concatenate([(j > s).astype(BF16), jnp.ones((LANES, LANES), BF16)], axis=1)
    return jnp.concatenate([u1, u1], axis=0)


def _later_pages_matrix(n_pages, n_heads):
    n = n_pages * n_heads
    a = lax.broadcasted_iota(jnp.int32, (n, n), 0)
    b = lax.broadcasted_iota(jnp.int32, (n, n), 1)
    same_head = (a % n_heads) == (b % n_heads)
    return jnp.logical_and(same_head, b // n_heads > a // n_heads).astype(BF16)


def _decode_attn(q_bf, k_new, v_new, cache_k, cache_v, page_table, lam_p, sub_g, *, layer, lam_init):
    bs, width = q_bf.shape
    page, n_heads = cache_k.shape[2], cache_k.shape[3]
    assert page == LANES and cache_k.shape[4] == HEAD_DIM
    n_pages = page_table.shape[1]
    pages = PAGES_PER_STEP
    n_steps = n_pages // pages
    ck = cache_k.transpose(0, 1, 3, 4, 2)
    cv = cache_v.transpose(0, 1, 3, 4, 2)
    col = lambda a: a.astype(F32).reshape(bs, n_heads, HEAD_DIM, 1)
    qb = jnp.broadcast_to(col(q_bf), (bs, n_heads, HEAD_DIM, LANES))
    lane0 = ((0, 0), (0, 0), (0, 0), (0, LANES - 1))
    kx = jnp.pad(col(k_new), lane0)
    vx = jnp.pad(col(v_new), lane0)

    def k_map(r):
        return lambda b, t, pt: (layer, pt[b, jnp.minimum(t, n_steps - 1) * pages + r], 0, 0, 0)

    def v_map(r):
        return lambda b, t, pt: (layer, pt[b, jnp.maximum(t - n_steps, 0) * pages + r], 0, 0, 0)

    seq_spec = lambda: pl.BlockSpec((None, n_heads, HEAD_DIM, LANES), lambda b, t, pt: (b, 0, 0, 0))
    const = lambda shape: pl.BlockSpec(shape, lambda b, t, pt: (0,) * len(shape))
    page_block = (None, None, n_heads, HEAD_DIM, LANES)
    n_rows = n_pages * n_heads
    in_specs = [seq_spec(), seq_spec(), seq_spec(),
                const(lam_p.shape), const((1, LANES)), const((2 * LANES, 2 * LANES)),
                const((n_rows, n_rows))]
    in_specs += [pl.BlockSpec(page_block, k_map(r)) for r in range(pages)]
    in_specs += [pl.BlockSpec(page_block, v_map(r)) for r in range(pages)]
    grid_spec = pltpu.PrefetchScalarGridSpec(
        num_scalar_prefetch=1,
        grid=(bs, 2 * n_steps),
        in_specs=in_specs,
        out_specs=pl.BlockSpec((None, 1, width), lambda b, t, pt: (b, 0, 0)),
        scratch_shapes=[
            pltpu.VMEM((n_pages, n_heads, LANES), F32),
            pltpu.VMEM((n_pages, n_heads, LANES), F32),
            pltpu.VMEM((n_heads, LANES), F32),
            pltpu.VMEM((n_heads, HEAD_DIM, LANES), F32),
        ],
    )
    out = pl.pallas_call(
        functools.partial(_dec_attn_kernel, n_steps=n_steps, pages=pages, n_heads=n_heads,
                          lam_init=lam_init),
        out_shape=jax.ShapeDtypeStruct((bs, 1, width), BF16),
        grid_spec=grid_spec,
        compiler_params=_cparams(("parallel", "arbitrary")),
        name="decode_attn",
    )(page_table, qb, kx, vx, lam_p, sub_g.reshape(1, LANES), _suffix_and_total_matrix(),
      _later_pages_matrix(n_pages, n_heads), *([ck] * pages), *([cv] * pages))
    return out.reshape(bs, width)


def _ffn_kernel(x_ref, g_ref, mod_ref, win_ref, cw_ref, cb_ref, wout_ref, fg_ref,
                y_ref, ff_ref, gs_ref, *, tiles_per_seq, tm, chunk, final):
    i = pl.program_id(0)
    seq = i // tiles_per_seq
    hidden = wout_ref.shape[0]
    halo = SUBLANES

    @pl.when(i % tiles_per_seq == 0)
    def _():
        gs_ref[0:halo, :] = jnp.zeros((halo, hidden), F32)

    x = x_ref[...]
    sh = _mod_row(mod_ref, 3, seq, False)
    sc = _mod_row(mod_ref, 4, seq, False)
    gate = _mod_row(mod_ref, 5, seq, False)
    h = _norm_mod(x, g_ref[...], sc, sh).astype(BF16)
    acc = jnp.zeros(x.shape, F32)
    for c in range(hidden // chunk):
        lo, up = c * chunk, (c + 1) * chunk
        g = jnp.dot(h, win_ref[:, lo:up], preferred_element_type=F32)
        u = jnp.dot(h, win_ref[:, hidden + lo:hidden + up], preferred_element_type=F32)
        gs_ref[halo:halo + tm, lo:up] = g
        g1 = gs_ref[halo - 1:halo - 1 + tm, lo:up]
        g2 = gs_ref[halo - 2:halo - 2 + tm, lo:up]
        gc = cw_ref[0:1, lo:up] * g2 + cw_ref[1:2, lo:up] * g1 + cw_ref[2:3, lo:up] * g + cb_ref[:, lo:up]
        act = (_silu(gc) * u).astype(BF16)
        acc = acc + jnp.dot(act, wout_ref[lo:up, :], preferred_element_type=F32)
    ff_ref[...] = gs_ref[halo + tm - (FFN_CONV - 1):halo + tm, :]
    gs_ref[0:halo, :] = gs_ref[tm:tm + halo, :]
    y = x + gate * acc
    if final:
        ms = jnp.mean(y * y, axis=-1, keepdims=True)
        y = y * lax.rsqrt(ms + EPS) * fg_ref[...]
    y_ref[...] = y


def _ffn_prompt(x, norm_g, mod, win_bf, conv_w, conv_b, wout_bf, final_g, *, batch, seq_len, final):
    m, d = x.shape
    hidden = wout_bf.shape[0]
    tm = TM_FFN
    tps = seq_len // tm
    return pl.pallas_call(
        functools.partial(_ffn_kernel, tiles_per_seq=tps, tm=tm, chunk=FFN_CHUNK, final=final),
        out_shape=(jax.ShapeDtypeStruct((m, d), F32),
                   jax.ShapeDtypeStruct((batch, FFN_CONV - 1, hidden), F32)),
        grid=(m // tm,),
        in_specs=[
            pl.BlockSpec((tm, d), lambda i: (i, 0)),
            _resident((1, d)),
            _resident(mod.shape),
            _resident(win_bf.shape),
            _resident(conv_w.shape),
            _resident((1, hidden)),
            _resident(wout_bf.shape),
            _resident((1, d)),
        ],
        out_specs=(pl.BlockSpec((tm, d), lambda i: (i, 0)),
                   pl.BlockSpec((None, FFN_CONV - 1, hidden), lambda i: (i // tps, 0, 0))),
        scratch_shapes=[pltpu.VMEM((tm + SUBLANES, hidden), F32)],
        compiler_params=_cparams(("arbitrary",)),
        name="conv_ffn",
    )(x, norm_g.reshape(1, d), mod, win_bf, conv_w, conv_b.reshape(1, hidden), wout_bf,
      final_g.reshape(1, d))


def _ffn_dec_kernel(x_ref, g_ref, mod_ref, wg_ref, wu_ref, past_ref, cw_ref, cb_ref, wout_ref, fg_ref,
                    y_ref, gnew_ref, h_scr, acc_scr, *, final):
    c = pl.program_id(0)

    @pl.when(c == 0)
    def _():
        h = _norm_mod(x_ref[...], g_ref[...], mod_ref[4], mod_ref[3])
        h_scr[...] = h.astype(BF16)
        acc_scr[...] = jnp.zeros_like(acc_scr)

    h = h_scr[...]
    g = jnp.dot(h, wg_ref[...], preferred_element_type=F32)
    u = jnp.dot(h, wu_ref[...], preferred_element_type=F32)
    gnew_ref[...] = g
    gc = cw_ref[0:1, :] * past_ref[0] + cw_ref[1:2, :] * past_ref[1] + cw_ref[2:3, :] * g + cb_ref[...]
    act = (_silu(gc) * u).astype(BF16)
    acc_scr[...] = acc_scr[...] + jnp.dot(act, wout_ref[...], preferred_element_type=F32)

    @pl.when(c == pl.num_programs(0) - 1)
    def _():
        y = x_ref[...] + mod_ref[5] * acc_scr[...]
        if final:
            ms = jnp.mean(y * y, axis=-1, keepdims=True)
            y = y * lax.rsqrt(ms + EPS) * fg_ref[...]
        y_ref[...] = y


def _ffn_decode(x, norm_g, mod, win_bf, past_t, conv_w, conv_b, wout_bf, final_g, *, final):
    m, d = x.shape
    hidden = wout_bf.shape[0]
    chunk = FFN_CHUNK_DEC
    n = hidden // chunk
    return pl.pallas_call(
        functools.partial(_ffn_dec_kernel, final=final),
        out_shape=(jax.ShapeDtypeStruct((m, d), F32), jax.ShapeDtypeStruct((m, hidden), F32)),
        grid=(n,),
        in_specs=[
            pl.BlockSpec((m, d), lambda c: (0, 0)),
            pl.BlockSpec((1, d), lambda c: (0, 0)),
            pl.BlockSpec(mod.shape, lambda c: (0, 0, 0)),
            pl.BlockSpec((d, chunk), lambda c: (0, c)),
            pl.BlockSpec((d, chunk), lambda c: (0, n + c)),
            pl.BlockSpec((FFN_CONV - 1, m, chunk), lambda c: (0, 0, c)),
            pl.BlockSpec((FFN_CONV, chunk), lambda c: (0, c)),
            pl.BlockSpec((1, chunk), lambda c: (0, c)),
            pl.BlockSpec((chunk, d), lambda c: (c, 0)),
            pl.BlockSpec((1, d), lambda c: (0, 0)),
        ],
        out_specs=(pl.BlockSpec((m, d), lambda c: (0, 0)),
                   pl.BlockSpec((m, chunk), lambda c: (0, c))),
        scratch_shapes=[pltpu.VMEM((m, d), BF16), pltpu.VMEM((m, d), F32)],
        compiler_params=_cparams(("arbitrary",)),
        name="conv_ffn_decode",
    )(x, norm_g.reshape(1, d), mod, win_bf, win_bf, past_t, conv_w, conv_b.reshape(1, hidden),
      wout_bf, final_g.reshape(1, d))


def _mixer_kernel(x_ref, g_ref, mod_ref, win_ref, clg_ref, clb_ref, ws_ref, bs_ref, dw_ref, db_ref,
                  dlg_ref, dlb_ref, wout_ref, y_ref, cv_ref, cd_ref, a_scr,
                  *, tiles_per_seq, tm, width):
    i = pl.program_id(0)
    seq = i // tiles_per_seq
    halo = 4 * SUBLANES
    taps = D_CONV

    @pl.when(i % tiles_per_seq == 0)
    def _():
        a_scr[0:halo, :] = jnp.zeros((halo, width), F32)

    x = x_ref[...]
    sh = _mod_row(mod_ref, 0, seq, False)
    sc = _mod_row(mod_ref, 1, seq, False)
    gate = _mod_row(mod_ref, 2, seq, False)
    h = _norm_mod(x, g_ref[...], sc, sh).astype(BF16)

    u = _gelu_tanh(jnp.dot(h, win_ref[:, 0:width], preferred_element_type=F32))
    v = _gelu_tanh(jnp.dot(h, win_ref[:, width:2 * width], preferred_element_type=F32))
    v = _layer_norm(v, clg_ref[...], clb_ref[...])
    cv_ref[...] = v[tm - CHUNK:tm, :]
    vb = v.astype(BF16)
    gw = width // C_GROUPS
    r = lax.broadcasted_iota(jnp.int32, (CHUNK, CHUNK), 0)
    s = lax.broadcasted_iota(jnp.int32, (CHUNK, CHUNK), 1)
    mixed_rows = []
    for ch in range(tm // CHUNK):
        cols = []
        for grp in range(C_GROUPS):
            wm = jnp.where(s <= r, ws_ref[grp], 0.0).astype(BF16)
            vg = vb[ch * CHUNK:(ch + 1) * CHUNK, grp * gw:(grp + 1) * gw]
            cols.append(jnp.dot(wm, vg, preferred_element_type=F32))
        mixed_rows.append(jnp.concatenate(cols, axis=1) + bs_ref[...])
    s_out = u * jnp.concatenate(mixed_rows, axis=0)

    za = jnp.dot(h, win_ref[:, 2 * width:3 * width], preferred_element_type=F32)
    zb = jnp.dot(h, win_ref[:, 3 * width:4 * width], preferred_element_type=F32)
    a = za * jax.nn.sigmoid(zb)
    a_scr[halo:halo + tm, :] = a
    base = halo - (taps - 1)
    dc = jnp.zeros((tm, width), F32) + db_ref[...]
    for k in range(taps):
        dc = dc + dw_ref[k:k + 1, :] * a_scr[base + k:base + k + tm, :]
    cd_ref[...] = a_scr[halo + tm - (taps - 1):halo + tm, :]
    a_scr[0:halo, :] = a_scr[tm:tm + halo, :]
    d_out = _silu(_layer_norm(dc, dlg_ref[...], dlb_ref[...]))

    o = jnp.dot(s_out.astype(BF16), wout_ref[0:width, :], preferred_element_type=F32)
    o = o + jnp.dot(d_out.astype(BF16), wout_ref[width:2 * width, :], preferred_element_type=F32)
    y_ref[...] = x + gate * o


def _mixer_prompt(x, norm_g, mod, win_bf, c_ln_g, c_ln_b, ws, bs, dw, db, d_ln_g, d_ln_b, wout_bf,
                  *, batch, seq_len):
    m, d = x.shape
    width = win_bf.shape[1] // 4
    tm = TM_MIX
    tps = seq_len // tm
    gw = width // C_GROUPS
    bs_exp = jnp.repeat(bs.T, gw, axis=1)
    row = lambda a: a.reshape(1, width)
    return pl.pallas_call(
        functools.partial(_mixer_kernel, tiles_per_seq=tps, tm=tm, width=width),
        out_shape=(jax.ShapeDtypeStruct((m, d), F32),
                   jax.ShapeDtypeStruct((batch, CHUNK, width), F32),
                   jax.ShapeDtypeStruct((batch, D_CONV - 1, width), F32)),
        grid=(m // tm,),
        in_specs=[
            pl.BlockSpec((tm, d), lambda i: (i, 0)),
            _resident((1, d)),
            _resident(mod.shape),
            _resident(win_bf.shape),
            _resident((1, width)), _resident((1, width)),
            _resident(ws.shape),
            _resident(bs_exp.shape),
            _resident(dw.shape),
            _resident((1, width)), _resident((1, width)), _resident((1, width)),
            _resident(wout_bf.shape),
        ],
        out_specs=(pl.BlockSpec((tm, d), lambda i: (i, 0)),
                   pl.BlockSpec((None, CHUNK, width), lambda i: (i // tps, 0, 0)),
                   pl.BlockSpec((None, D_CONV - 1, width), lambda i: (i // tps, 0, 0))),
        scratch_shapes=[pltpu.VMEM((tm + 4 * SUBLANES, width), F32)],
        compiler_params=_cparams(("arbitrary",)),
        name="mixer",
    )(x, norm_g.reshape(1, d), mod, win_bf, row(c_ln_g), row(c_ln_b), ws, bs_exp, dw, row(db),
      row(d_ln_g), row(d_ln_b), wout_bf)


def _mixer_dec_kernel(x_ref, g_ref, mod_ref, win_ref, clg_ref, clb_ref, w0_ref, b0_ref, past_ref,
                      dw_ref, db_ref, dlg_ref, dlb_ref, wout_ref, y_ref, v_ref, a_ref, *, width):
    x = x_ref[...]
    h = _norm_mod(x, g_ref[...], mod_ref[1], mod_ref[0]).astype(BF16)
    u = _gelu_tanh(jnp.dot(h, win_ref[:, 0:width], preferred_element_type=F32))
    v = _gelu_tanh(jnp.dot(h, win_ref[:, width:2 * width], preferred_element_type=F32))
    v = _layer_norm(v, clg_ref[...], clb_ref[...])
    v_ref[...] = v
    mixed = w0_ref[...] * v + b0_ref[...]
    s_out = u * mixed
    za = jnp.dot(h, win_ref[:, 2 * width:3 * width], preferred_element_type=F32)
    zb = jnp.dot(h, win_ref[:, 3 * width:4 * width], preferred_element_type=F32)
    a = za * jax.nn.sigmoid(zb)
    a_ref[...] = a
    dc = dw_ref[D_CONV - 1:D_CONV, :] * a + db_ref[...]
    for k in range(D_CONV - 1):
        dc = dc + dw_ref[k:k + 1, :] * past_ref[k]
    d_out = _silu(_layer_norm(dc, dlg_ref[...], dlb_ref[...]))
    o = jnp.dot(s_out.astype(BF16), wout_ref[0:width, :], preferred_element_type=F32)
    o = o + jnp.dot(d_out.astype(BF16), wout_ref[width:2 * width, :], preferred_element_type=F32)
    y_ref[...] = x + mod_ref[2] * o


def _mixer_decode(x, norm_g, mod, win_bf, c_ln_g, c_ln_b, ws, bs, past_t, dw, db, d_ln_g, d_ln_b, wout_bf):
    m, d = x.shape
    width = win_bf.shape[1] // 4
    gw = width // C_GROUPS
    row = lambda a: a.reshape(1, width)
    w0 = jnp.repeat(ws[:, 0, 0], gw).reshape(1, width)
    b0 = jnp.repeat(bs[:, 0], gw).reshape(1, width)
    args = (x, norm_g.reshape(1, d), mod, win_bf, row(c_ln_g), row(c_ln_b), w0, b0, past_t, dw, row(db),
            row(d_ln_g), row(d_ln_b), wout_bf)
    full = lambda a: pl.BlockSpec(a.shape, lambda i, nd=a.ndim: (0,) * nd)
    return pl.pallas_call(
        functools.partial(_mixer_dec_kernel, width=width),
        out_shape=(jax.ShapeDtypeStruct((m, d), F32),
                   jax.ShapeDtypeStruct((m, width), F32),
                   jax.ShapeDtypeStruct((m, width), F32)),
        grid=(1,),
        in_specs=[full(a) for a in args],
        out_specs=(pl.BlockSpec((m, d), lambda i: (0, 0)),
                   pl.BlockSpec((m, width), lambda i: (0, 0)),
                   pl.BlockSpec((m, width), lambda i: (0, 0))),
        compiler_params=_cparams(("arbitrary",)),
        name="mixer_decode",
    )(*args)


def kernel(x_prompt, x_sample, c_prompt, c_sample, cache_k, cache_v, state_conv_d, state_ffn_conv,
           page_table, norm_g, ada_w, ada_b, att_w_in, att_lam, att_sub_g, att_w_out, mix_w_in,
           c_ln_g, c_ln_b, c_ws, c_bs, d_conv_w, d_conv_b, d_ln_g, d_ln_b, mix_w_out,
           ffn_w_in, ffn_conv_w, ffn_conv_b, ffn_w_out, final_g):
    batch, seq_len, d = x_prompt.shape
    bs = x_sample.shape[0]
    depth = ada_w.shape[0]
    att_width = att_w_out.shape[1]
    n_heads = att_width // HEAD_DIM
    hidden = ffn_w_out.shape[1]
    assert x_sample.shape[1] == 1 and seq_len % TQ == 0 and seq_len % TM_PROJ == 0
    assert bs % (2 * SUBLANES) == 0 and page_table.shape[1] % PAGES_PER_STEP == 0

    rows = bs + batch
    pad = (-rows) % (2 * SUBLANES)
    c_all = jnp.concatenate([c_sample, c_prompt, jnp.zeros((pad, d), F32)], axis=0)
    mod = _ada_mod(c_all, ada_w, ada_b)
    mod_s = mod[:, :, 0:bs]
    mod_p = mod[:, :, bs:bs + batch]

    xp = x_prompt.reshape(batch * seq_len, d)
    xs = x_sample.reshape(bs, d)
    k_p, v_p, k_s, v_s, cv_p, cv_s, cd_p, cd_s, ff_p, ff_s = ([] for _ in range(10))

    for i in range(depth):
        j = i // 2
        if i % 2 == 0:
            lam_init = 0.8 - 0.6 * math.exp(-0.3 * i)
            w_in = att_w_in[j].astype(BF16)
            w_out = att_w_out[j].astype(BF16)
            n_sb = N_SB_HEADS * HEAD_DIM
            q, k, v, kb, vb = _qkv_proj(xp, norm_g[i, 0], mod_p[i], w_in, tm=TM_PROJ,
                                        tiles_per_seq=seq_len // TM_PROJ, per_row=False)
            o_sb = _attn_sb(q, kb, vb, batch=batch, seq_len=seq_len, tq=TQ)
            o_d = _attn_diff(q, kb, vb, att_lam[j], att_sub_g[j], batch=batch, seq_len=seq_len, tq=TQ,
                             lam_init=lam_init)
            xp = _out_proj(xp, o_sb, o_d, mod_p[i], w_out, tm=TM_PROJ,
                           tiles_per_seq=seq_len // TM_PROJ, per_row=False)
            k_p.append(k.reshape(batch, seq_len, n_heads, HEAD_DIM))
            v_p.append(v.reshape(batch, seq_len, n_heads, HEAD_DIM))
            q, k, v, _, _ = _qkv_proj(xs, norm_g[i, 0], mod_s[i], w_in, tm=bs, tiles_per_seq=1, per_row=True)
            o = _decode_attn(q, k, v, cache_k, cache_v, page_table, att_lam[j], att_sub_g[j],
                             layer=j, lam_init=lam_init)
            xs = _out_proj(xs, o[:, 0:n_sb], o[:, n_sb:], mod_s[i], w_out, tm=bs, tiles_per_seq=1,
                           per_row=True)
            k_s.append(k.reshape(bs, 1, n_heads, HEAD_DIM))
            v_s.append(v.reshape(bs, 1, n_heads, HEAD_DIM))
        else:
            w_in = mix_w_in[j].astype(BF16)
            w_out = mix_w_out[j].astype(BF16)
            xp, cv, cd = _mixer_prompt(xp, norm_g[i, 0], mod_p[i], w_in, c_ln_g[j], c_ln_b[j], c_ws[j],
                                       c_bs[j], d_conv_w[j], d_conv_b[j], d_ln_g[j], d_ln_b[j], w_out,
                                       batch=batch, seq_len=seq_len)
            cv_p.append(cv)
            cd_p.append(cd)
            past = state_conv_d[j]
            xs, v_new, a_new = _mixer_decode(xs, norm_g[i, 0], mod_s[i], w_in, c_ln_g[j], c_ln_b[j],
                                             c_ws[j], c_bs[j], past.transpose(1, 0, 2), d_conv_w[j],
                                             d_conv_b[j], d_ln_g[j], d_ln_b[j], w_out)
            cv_s.append(v_new[:, None, :])
            cd_s.append(jnp.concatenate([past[:, 1:], a_new[:, None, :]], axis=1))
        final = i == depth - 1
        w_in = ffn_w_in[i].astype(BF16)
        w_out = ffn_w_out[i].astype(BF16)
        xp, ff = _ffn_prompt(xp, norm_g[i, 1], mod_p[i], w_in, ffn_conv_w[i], ffn_conv_b[i], w_out,
                             final_g, batch=batch, seq_len=seq_len, final=final)
        ff_p.append(ff)
        past = state_ffn_conv[i]
        xs, g_new = _ffn_decode(xs, norm_g[i, 1], mod_s[i], w_in, past.transpose(1, 0, 2), ffn_conv_w[i],
                                ffn_conv_b[i], w_out, final_g, final=final)
        ff_s.append(jnp.concatenate([past[:, 1:], g_new[:, None, :]], axis=1))

    return (xp.reshape(batch, seq_len, d), xs.reshape(bs, 1, d),
            jnp.stack(k_p), jnp.stack(v_p), jnp.stack(k_s), jnp.stack(v_s),
            jnp.stack(cv_p), jnp.stack(cv_s), jnp.stack(cd_p), jnp.stack(cd_s),
            jnp.stack(ff_p), jnp.stack(ff_s))
```

```python
import functools
import math

import jax
import jax.numpy as jnp
from jax import lax
from jax.experimental import pallas as pl
from jax.experimental.pallas import tpu as pltpu

F32 = jnp.float32
BF16 = jnp.bfloat16

HEAD_DIM = 64
N_SB_HEADS = 8
N_DIFF_HEADS = 4
N_MOD = 6
C_GROUPS = 4
CHUNK = 128
D_CONV = 31
FFN_CONV = 3
EPS = 1e-6
NEG_INF = -1e30

LANES = 128
SUBLANES = 8
VMEM_LIMIT = 56 * 1024 * 1024

TM_PROJ = 512
TM_FFN = 256
TM_MIX = 256
TQ = 256
PAGES_PER_STEP = 16
FFN_CHUNK = 256
FFN_CHUNK_DEC = 1408


def _cparams(sem, vmem=VMEM_LIMIT):
    return pltpu.CompilerParams(dimension_semantics=sem, vmem_limit_bytes=vmem)


def _resident(shape):
    nd = len(shape)
    return pl.BlockSpec(shape, lambda *_: (0,) * nd, pipeline_mode=pl.Buffered(1))


def _norm_mod(x, g, sc, sh):
    ms = jnp.mean(x * x, axis=-1, keepdims=True)
    y = x * lax.rsqrt(ms + EPS) * g
    return y * (1.0 + sc) + sh


def _mod_row(mod_ref, slot, seq, per_row):
    if per_row:
        return mod_ref[slot]
    return mod_ref[slot, pl.ds(seq, 1), :]


def _layer_norm(x, g, b):
    mu = jnp.mean(x, axis=-1, keepdims=True)
    xc = x - mu
    var = jnp.mean(xc * xc, axis=-1, keepdims=True)
    return xc * lax.rsqrt(var + EPS) * g + b


def _gelu_tanh(x):
    c = math.sqrt(2.0 / math.pi)
    return x * (0.5 * (1.0 + jnp.tanh(c * (x + 0.044715 * (x * x * x)))))


def _silu(x):
    return x * jax.nn.sigmoid(x)


def _softplus(z):
    return jnp.maximum(z, 0.0) + jnp.log(1.0 + jnp.exp(-jnp.abs(z)))


def _split_bf16(x):
    hi = x.astype(BF16)
    lo = (x - hi.astype(F32)).astype(BF16)
    return hi, lo


def _ada_kernel(c_ref, w_ref, b_ref, o_ref):
    c = c_ref[...]
    s = _silu(c).astype(BF16)
    w = w_ref[...].astype(BF16)
    o_ref[...] = jnp.dot(s, w, preferred_element_type=F32) + b_ref[...]


def _ada_mod(c_all, ada_w, ada_b):
    depth, d, _ = ada_w.shape
    rows = c_all.shape[0]
    return pl.pallas_call(
        _ada_kernel,
        out_shape=jax.ShapeDtypeStruct((depth, N_MOD, rows, d), F32),
        grid=(depth, N_MOD),
        in_specs=[
            pl.BlockSpec((rows, d), lambda i, j: (0, 0)),
            pl.BlockSpec((None, d, d), lambda i, j: (i, 0, j)),
            pl.BlockSpec((None, None, 1, d), lambda i, j: (i, j, 0, 0)),
        ],
        out_specs=pl.BlockSpec((None, None, rows, d), lambda i, j: (i, j, 0, 0)),
        compiler_params=_cparams(("parallel", "parallel")),
        name="ada_mod",
    )(c_all, ada_w, ada_b.reshape(depth, N_MOD, 1, d))


def _qkv_dec_kernel(x_ref, g_ref, mod_ref, w_ref, q_ref, k_ref, v_ref, *, width):
    h = _norm_mod(x_ref[...], g_ref[...], mod_ref[1], mod_ref[0]).astype(BF16)
    q = jnp.dot(h, w_ref[:, 0:width], preferred_element_type=F32)
    q_ref[...] = (q * (HEAD_DIM ** -0.5)).astype(BF16)
    k_ref[...] = jnp.dot(h, w_ref[:, width:2 * width], preferred_element_type=F32)
    v_ref[...] = jnp.dot(h, w_ref[:, 2 * width:3 * width], preferred_element_type=F32)


def _qkv_decode(x, norm_g, mod, w_bf):
    m, d = x.shape
    width = w_bf.shape[1] // 3
    tile = lambda dt: jax.ShapeDtypeStruct((m, width), dt)
    full = lambda shape: pl.BlockSpec(shape, lambda i: (0,) * len(shape))
    return pl.pallas_call(
        functools.partial(_qkv_dec_kernel, width=width),
        out_shape=(tile(BF16), tile(F32), tile(F32)),
        grid=(1,),
        in_specs=[full((m, d)), full((1, d)), full(mod.shape), full(w_bf.shape)],
        out_specs=(full((m, width)),) * 3,
        compiler_params=_cparams(("arbitrary",)),
        name="qkv_decode",
    )(x, norm_g.reshape(1, d), mod, w_bf)


def _qkv_t_kernel(x_ref, g_ref, mod_ref, wq_ref, wkt_ref, wvt_ref, *rest, tiles_per_seq, n_prev):
    if n_prev:
        pk_ref, pv_ref = rest[0:2]
        rest = rest[2:]
    q_ref, kt_ref, vt_ref, ktb_ref, vtb_ref = rest
    seq = pl.program_id(0) // tiles_per_seq
    sh = _mod_row(mod_ref, 0, seq, False)
    sc = _mod_row(mod_ref, 1, seq, False)
    h = _norm_mod(x_ref[...], g_ref[...], sc, sh).astype(BF16)
    q = jnp.dot(h, wq_ref[...], preferred_element_type=F32)
    q_ref[...] = (q * (HEAD_DIM ** -0.5)).astype(BF16)
    nt = (((1,), (1,)), ((), ()))
    kt = lax.dot_general(wkt_ref[...], h, nt, preferred_element_type=F32)
    vt = lax.dot_general(wvt_ref[...], h, nt, preferred_element_type=F32)
    if n_prev:
        kt_ref[0:n_prev] = pk_ref[...]
        vt_ref[0:n_prev] = pv_ref[...]
    kt_ref[n_prev] = kt
    vt_ref[n_prev] = vt
    ktb_ref[...] = kt.astype(BF16)
    vtb_ref[...] = vt.astype(BF16)


def _qkv_prompt(x, norm_g, mod, wq, wkt, wvt, prev_k, prev_v, *, batch, seq_len, tm):
    m, d = x.shape
    width = wq.shape[1]
    tps = seq_len // tm
    n_prev = 0 if prev_k is None else prev_k.shape[0]
    slab = lambda n: pl.BlockSpec((n, None, width, tm), lambda i: (0, i // tps, 0, i % tps))
    blocked = pl.BlockSpec((None, None, width, tm), lambda i: (i // tps, i % tps, 0, 0))
    in_specs = [
        pl.BlockSpec((tm, d), lambda i: (i, 0)),
        _resident((1, d)),
        _resident(mod.shape),
        _resident(wq.shape), _resident(wkt.shape), _resident(wvt.shape),
    ]
    args = [x, norm_g.reshape(1, d), mod, wq, wkt, wvt]
    if n_prev:
        in_specs += [slab(n_prev), slab(n_prev)]
        args += [prev_k, prev_v]
    stacked = jax.ShapeDtypeStruct((n_prev + 1, batch, width, seq_len), F32)
    blk = jax.ShapeDtypeStruct((batch, tps, width, tm), BF16)
    return pl.pallas_call(
        functools.partial(_qkv_t_kernel, tiles_per_seq=tps, n_prev=n_prev),
        out_shape=(jax.ShapeDtypeStruct((m, width), BF16), stacked, stacked, blk, blk),
        grid=(m // tm,),
        in_specs=in_specs,
        out_specs=(pl.BlockSpec((tm, width), lambda i: (i, 0)), slab(n_prev + 1), slab(n_prev + 1),
                   blocked, blocked),
        compiler_params=_cparams(("parallel",)),
        name="qkv_prompt",
    )(*args)


def _oproj_kernel(x_ref, oa_ref, ob_ref, mod_ref, w_ref, y_ref, *, tiles_per_seq, per_row, split):
    seq = pl.program_id(0) // tiles_per_seq
    gate = _mod_row(mod_ref, 2, seq, per_row)
    o = jnp.dot(oa_ref[...], w_ref[0:split, :], preferred_element_type=F32)
    o = o + jnp.dot(ob_ref[...], w_ref[split:, :], preferred_element_type=F32)
    y_ref[...] = x_ref[...] + gate * o


def _out_proj(x, oa, ob, mod, w_bf, *, tm, tiles_per_seq, per_row):
    m, d = x.shape
    split = oa.shape[1]
    return pl.pallas_call(
        functools.partial(_oproj_kernel, tiles_per_seq=tiles_per_seq, per_row=per_row, split=split),
        out_shape=jax.ShapeDtypeStruct((m, d), F32),
        grid=(m // tm,),
        in_specs=[
            pl.BlockSpec((tm, d), lambda i: (i, 0)),
            pl.BlockSpec((tm, split), lambda i: (i, 0)),
            pl.BlockSpec((tm, ob.shape[1]), lambda i: (i, 0)),
            _resident(mod.shape),
            _resident(w_bf.shape),
        ],
        out_specs=pl.BlockSpec((tm, d), lambda i: (i, 0)),
        compiler_params=_cparams(("parallel",)),
        name="out_proj",
    )(x, oa, ob, mod, w_bf)


_NT = (((1,), (1,)), ((), ()))


def _sb_kernel(q_ref, kt_ref, vt_ref, uu_ref, o_ref, *, tq):
    i = pl.program_id(2)
    q = q_ref[...]
    lane = lax.broadcasted_iota(jnp.int32, (1, LANES), 1)
    row = lax.broadcasted_iota(jnp.int32, (tq, tq), 0)
    col = lax.broadcasted_iota(jnp.int32, (tq, tq), 1)
    strict = col < row
    uu = uu_ref[...]
    zero_q = jnp.zeros_like(q)
    qs = (jnp.where(lane < HEAD_DIM, q, zero_q), jnp.where(lane >= HEAD_DIM, q, zero_q))

    def blocks(js, state, diag):
        kbs = [kt_ref[j] for j in js]
        vbs = [vt_ref[j] for j in js]
        chains = [(n, h) for n in range(len(js)) for h in range(2)]
        z = {c: jnp.dot(qs[c[1]], kbs[c[0]], preferred_element_type=F32) for c in chains}
        sp = {c: _softplus(z[c]) for c in chains}
        lk = {c: jnp.where(strict, sp[c], 0.0) if (diag and c[0] == 0) else sp[c] for c in chains}
        cs = {c: jnp.dot(jnp.concatenate(_split_bf16(lk[c]), axis=1), uu, preferred_element_type=F32)
              for c in chains}
        tot = {c: jnp.sum(lk[c], axis=1, keepdims=True) for c in chains}
        new = []
        for h in range(2):
            carry, acc = state[2 * h], state[2 * h + 1]
            for n in range(len(js)):
                c = (n, h)
                a = jnp.exp(z[c] - sp[c] - cs[c] - carry)
                if diag and n == 0:
                    a = jnp.where(strict, a, 0.0)
                acc = acc + lax.dot_general(a.astype(BF16), vbs[n], _NT, preferred_element_type=F32)
                carry = carry + tot[c]
            new += [carry, acc]
        return tuple(new)

    c0 = jnp.zeros((tq, 1), F32)
    a0 = jnp.zeros((tq, LANES), F32)
    state = (c0, a0, c0, a0)
    has_prev = jnp.minimum(i, 1)
    state = lax.fori_loop(0, 1 - has_prev, lambda t, st: blocks([i], st, True), state)
    state = lax.fori_loop(0, has_prev, lambda t, st: blocks([i, i - 1], st, True), state)
    rest = jnp.maximum(i - 1, 0)
    state = lax.fori_loop(0, rest // 2, lambda t, st: blocks([i - 2 - 2 * t, i - 3 - 2 * t], st, False), state)
    state = lax.fori_loop(0, rest % 2, lambda t, st: blocks([0], st, False), state)
    o_ref[...] = jnp.where(lane < HEAD_DIM, state[1], state[3]).astype(o_ref.dtype)


def _suffix_matrix(n):
    j = lax.broadcasted_iota(jnp.int32, (n, n), 0)
    s = lax.broadcasted_iota(jnp.int32, (n, n), 1)
    u = (j > s).astype(BF16)
    return jnp.concatenate([u, u], axis=0)


def _attn_sb(q_bf, ktb, vtb, *, batch, seq_len, tq):
    m = q_bf.shape[0]
    nq = seq_len // tq
    n_pairs = N_SB_HEADS * HEAD_DIM // LANES
    kv_spec = pl.BlockSpec((None, nq, LANES, tq), lambda b, p, i: (b, 0, p, 0))
    return pl.pallas_call(
        functools.partial(_sb_kernel, tq=tq),
        out_shape=jax.ShapeDtypeStruct((m, n_pairs * LANES), BF16),
        grid=(batch, n_pairs, nq),
        in_specs=[
            pl.BlockSpec((tq, LANES), lambda b, p, i: (b * nq + i, p)),
            kv_spec, kv_spec,
            _resident((2 * tq, tq)),
        ],
        out_specs=pl.BlockSpec((tq, LANES), lambda b, p, i: (b * nq + i, p)),
        compiler_params=_cparams(("parallel", "parallel", "parallel")),
        name="attn_sb",
    )(q_bf, ktb, vtb, _suffix_matrix(tq))


def _lam_value(lam_ref, lam_init):
    lp = lam_ref[...]
    t1 = jnp.sum(lp[0:1] * lp[1:2], axis=1, keepdims=True)
    t2 = jnp.sum(lp[2:3] * lp[3:4], axis=1, keepdims=True)
    return jnp.exp(t1) - jnp.exp(t2) + lam_init


def _diff_kernel(lam_ref, g_ref, q_ref, kt_ref, vt_ref, o_ref, vext_ref, *, tq, lam_init):
    i = pl.program_id(2)

    @pl.when(i == 0)
    def _():
        vext_ref[:, 0:LANES, :] = vt_ref[...]
        vext_ref[:, LANES:2 * LANES, :] = jnp.ones(vt_ref.shape, BF16)

    q = q_ref[...]
    lane = lax.broadcasted_iota(jnp.int32, (1, LANES), 1)
    row = lax.broadcasted_iota(jnp.int32, (tq, tq), 0)
    col = lax.broadcasted_iota(jnp.int32, (tq, tq), 1)
    causal = col <= row
    zero_q = jnp.zeros_like(q)
    qs = (jnp.where(lane < HEAD_DIM, q, zero_q), jnp.where(lane >= HEAD_DIM, q, zero_q))

    def blocks(js, state, diag):
        kbs = [kt_ref[j] for j in js]
        ves = [vext_ref[j] for j in js]
        s = {(n, c): jnp.dot(qs[c], kbs[n], preferred_element_type=F32)
             for n in range(len(js)) for c in range(2)}
        new = []
        for c in range(2):
            m_old, acc = state[2 * c], state[2 * c + 1]
            sc = [jnp.where(causal, s[(n, c)], NEG_INF) if (diag and n == 0) else s[(n, c)]
                  for n in range(len(js))]
            m_new = m_old
            for n in range(len(js)):
                m_new = jnp.maximum(m_new, jnp.max(sc[n], axis=1, keepdims=True))
            acc = jnp.exp(m_old - m_new) * acc
            for n in range(len(js)):
                p = jnp.exp(sc[n] - m_new)
                acc = acc + lax.dot_general(p.astype(BF16), ves[n], _NT, preferred_element_type=F32)
            new += [m_new, acc]
        return tuple(new)

    m0 = jnp.full((tq, 1), NEG_INF, F32)
    a0 = jnp.zeros((tq, 2 * LANES), F32)
    state = (m0, a0, m0, a0)
    has_prev = jnp.minimum(i, 1)
    state = lax.fori_loop(0, 1 - has_prev, lambda t, st: blocks([i], st, True), state)
    state = lax.fori_loop(0, has_prev, lambda t, st: blocks([i, i - 1], st, True), state)
    rest = jnp.maximum(i - 1, 0)
    state = lax.fori_loop(0, rest // 2, lambda t, st: blocks([i - 2 - 2 * t, i - 3 - 2 * t], st, False), state)
    state = lax.fori_loop(0, rest % 2, lambda t, st: blocks([0], st, False), state)

    lam = _lam_value(lam_ref, lam_init)
    o1 = state[1][:, 0:LANES] / state[1][:, LANES:2 * LANES]
    o2 = state[3][:, 0:LANES] / state[3][:, LANES:2 * LANES]
    o = o1 - lam * o2
    ms = jnp.mean(o * o, axis=1, keepdims=True)
    o = o * lax.rsqrt(ms + EPS) * g_ref[...] * (1.0 - lam_init)
    o_ref[...] = o.astype(o_ref.dtype)


def _attn_diff(q_bf, ktb, vtb, lam_p, sub_g, *, batch, seq_len, tq, lam_init):
    m = q_bf.shape[0]
    nq = seq_len // tq
    off = N_SB_HEADS * HEAD_DIM // LANES
    kv_spec = pl.BlockSpec((None, nq, LANES, tq), lambda b, p, i: (b, 0, off + p, 0))
    return pl.pallas_call(
        functools.partial(_diff_kernel, tq=tq, lam_init=lam_init),
        out_shape=jax.ShapeDtypeStruct((m, N_DIFF_HEADS * LANES), BF16),
        grid=(batch, N_DIFF_HEADS, nq),
        in_specs=[
            _resident(lam_p.shape),
            _resident((1, LANES)),
            pl.BlockSpec((tq, LANES), lambda b, p, i: (b * nq + i, off + p)),
            kv_spec, kv_spec,
        ],
        out_specs=pl.BlockSpec((tq, LANES), lambda b, p, i: (b * nq + i, p)),
        scratch_shapes=[pltpu.VMEM((nq, 2 * LANES, tq), BF16)],
        compiler_params=_cparams(("parallel", "parallel", "arbitrary")),
        name="attn_diff",
    )(lam_p, sub_g.reshape(1, LANES), q_bf, ktb, vtb)


def _dec_attn_kernel(pt_ref, qt_ref, kx_ref, vx_ref, lam_ref, g_ref, uo_ref, msuf_ref,
                     *rest, n_steps, pages, n_heads, lam_init):
    k_refs = rest[0:pages]
    v_refs = rest[pages:2 * pages]
    o_ref = rest[2 * pages]
    s_scr, w_scr, x_scr, acc_scr, qb_scr = rest[2 * pages + 1:]
    del pt_ref
    seq = pl.program_id(0)
    t = pl.program_id(1)
    n_pages = n_steps * pages
    n_rows = n_pages * n_heads
    lane = lax.broadcasted_iota(jnp.int32, (1, LANES), 1)
    rid = lax.broadcasted_iota(jnp.int32, (n_heads, 1), 0)
    is_sb = rid < N_SB_HEADS

    def value_row(wp, h):
        if h < N_SB_HEADS:
            return wp[h:h + 1, :]
        first = N_SB_HEADS + 2 * ((h - N_SB_HEADS) // 2)
        return wp[first:first + 1, :] + wp[first + 1:first + 2, :]

    def q_head(h):
        return qb_scr[h * HEAD_DIM:(h + 1) * HEAD_DIM, :]

    @pl.when(t == 0)
    def _():
        acc_scr[...] = jnp.zeros_like(acc_scr)
        src = lax.broadcasted_iota(jnp.int32, (LANES, LANES), 0)
        pick = jnp.where(src == seq, 1.0, 0.0).astype(BF16)
        qb_scr[...] = jnp.dot(qt_ref[...], pick, preferred_element_type=F32)

    @pl.when(t < n_steps)
    def _():
        for h in range(n_heads):
            qh = q_head(h)
            for r in range(pages):
                row = jnp.sum(k_refs[r][h] * qh, axis=0, keepdims=True)
                s_scr[t * pages + r, pl.ds(h, 1), :] = row

    @pl.when(t == n_steps - 1)
    def _():
        for h in range(n_heads):
            x_scr[pl.ds(h, 1), :] = jnp.sum(kx_ref[h] * q_head(h), axis=0, keepdims=True)
        s3 = s_scr[...]
        s = s3.reshape(n_rows, LANES)
        sp = _softplus(s)
        hi, lo = _split_bf16(sp)
        r1 = jnp.dot(jnp.concatenate([hi, lo], axis=1), uo_ref[...], preferred_element_type=F32)
        cs = r1[:, 0:LANES]
        tot = r1[:, LANES:2 * LANES]
        t1 = tot.astype(BF16)
        rem = tot - t1.astype(F32)
        t2, t3 = _split_bf16(rem)
        c3 = jnp.dot(msuf_ref[...], jnp.concatenate([t1, t2, t3], axis=1), preferred_element_type=F32)
        carry = c3[:, 0:LANES] + c3[:, LANES:2 * LANES] + c3[:, 2 * LANES:3 * LANES]
        a3 = jnp.exp(s - sp - cs - carry).reshape(n_pages, n_heads, LANES)
        sx = jnp.where(lane == seq, x_scr[...], NEG_INF)
        m = jnp.max(jnp.maximum(jnp.max(s3, axis=0), sx), axis=1, keepdims=True)
        p3 = jnp.exp(s3 - m[None])
        px = jnp.exp(sx - m)
        l = jnp.sum(jnp.sum(p3, axis=0) + px, axis=1, keepdims=True)
        lam = _lam_value(lam_ref, lam_init)
        coef = jnp.where(lax.bitwise_and(rid, 1) == 0, 1.0, -lam) / l
        w_scr[...] = jnp.where(is_sb[None], a3, p3 * coef[None])
        x_scr[...] = jnp.where(is_sb, 0.0, px * coef)

    @pl.when(t >= n_steps)
    def _():
        blk = t - n_steps
        wps = [w_scr[blk * pages + r] for r in range(pages)]
        for h in range(n_heads):
            acc = acc_scr[h]
            for r in range(pages):
                acc = acc + value_row(wps[r], h) * v_refs[r][h]
            acc_scr[h] = acc

    @pl.when(t == 2 * n_steps - 1)
    def _():
        wx = x_scr[...]
        for h in range(n_heads):
            acc_scr[h] = acc_scr[h] + value_row(wx, h) * vx_ref[h]
        hi, lo = _split_bf16(acc_scr[...].reshape(n_heads * HEAD_DIM, LANES))
        ones = jnp.ones((SUBLANES, LANES), BF16)
        nt = (((1,), (1,)), ((), ()))
        o = lax.dot_general(ones, hi, nt, preferred_element_type=F32)
        o = (o + lax.dot_general(ones, lo, nt, preferred_element_type=F32))[0:1, :]
        n_sb = N_SB_HEADS * HEAD_DIM
        o_ref[:, 0:n_sb] = o[:, 0:n_sb].astype(o_ref.dtype)
        for hd in range(N_DIFF_HEADS):
            od = o[:, n_sb + hd * LANES:n_sb + (hd + 1) * LANES]
            ms = jnp.mean(od * od, axis=1, keepdims=True)
            od = od * lax.rsqrt(ms + EPS) * g_ref[...] * (1.0 - lam_init)
            o_ref[:, n_sb + hd * LANES:n_sb + (hd + 1) * LANES] = od.astype(o_ref.dtype)


def _suffix_and_total_matrix():
    j = lax.broadcasted_iota(jnp.int32, (LANES, LANES), 0)
    s = lax.broadcasted_iota(jnp.int32, (LANES, LANES), 1)
    u1 = jnp.concatenate([(j > s).astype(BF16), jnp.ones((LANES, LANES), BF16)], axis=1)
    return jnp.concatenate([u1, u1], axis=0)


def _later_pages_matrix(n_pages, n_heads):
    n = n_pages * n_heads
    a = lax.broadcasted_iota(jnp.int32, (n, n), 0)
    b = lax.broadcasted_iota(jnp.int32, (n, n), 1)
    same_head = (a % n_heads) == (b % n_heads)
    return jnp.logical_and(same_head, b // n_heads > a // n_heads).astype(BF16)


def _decode_attn(q_bf, k_new, v_new, cache_k, cache_v, page_table, lam_p, sub_g, *, layer, lam_init):
    bs, width = q_bf.shape
    page, n_heads = cache_k.shape[2], cache_k.shape[3]
    assert page == LANES and cache_k.shape[4] == HEAD_DIM
    n_pages = page_table.shape[1]
    pages = PAGES_PER_STEP
    n_steps = n_pages // pages
    ck = cache_k.transpose(0, 1, 3, 4, 2)
    cv = cache_v.transpose(0, 1, 3, 4, 2)
    assert bs <= LANES
    lanes = lambda a: jnp.pad(a.T, ((0, 0), (0, LANES - bs)))
    qt = lanes(q_bf)
    kx = lanes(k_new).reshape(n_heads, HEAD_DIM, LANES)
    vx = lanes(v_new).reshape(n_heads, HEAD_DIM, LANES)

    def k_map(r):
        return lambda b, t, pt: (layer, pt[b, jnp.minimum(t, n_steps - 1) * pages + r], 0, 0, 0)

    def v_map(r):
        return lambda b, t, pt: (layer, pt[b, jnp.maximum(t - n_steps, 0) * pages + r], 0, 0, 0)

    const = lambda shape: pl.BlockSpec(shape, lambda b, t, pt: (0,) * len(shape))
    page_block = (None, None, n_heads, HEAD_DIM, LANES)
    n_rows = n_pages * n_heads
    in_specs = [const(qt.shape), const(kx.shape), const(vx.shape),
                const(lam_p.shape), const((1, LANES)), const((2 * LANES, 2 * LANES)),
                const((n_rows, n_rows))]
    in_specs += [pl.BlockSpec(page_block, k_map(r)) for r in range(pages)]
    in_specs += [pl.BlockSpec(page_block, v_map(r)) for r in range(pages)]
    grid_spec = pltpu.PrefetchScalarGridSpec(
        num_scalar_prefetch=1,
        grid=(bs, 2 * n_steps),
        in_specs=in_specs,
        out_specs=pl.BlockSpec((None, 1, width), lambda b, t, pt: (b, 0, 0)),
        scratch_shapes=[
            pltpu.VMEM((n_pages, n_heads, LANES), F32),
            pltpu.VMEM((n_pages, n_heads, LANES), F32),
            pltpu.VMEM((n_heads, LANES), F32),
            pltpu.VMEM((n_heads, HEAD_DIM, LANES), F32),
            pltpu.VMEM((width, LANES), F32),
        ],
    )
    out = pl.pallas_call(
        functools.partial(_dec_attn_kernel, n_steps=n_steps, pages=pages, n_heads=n_heads,
                          lam_init=lam_init),
        out_shape=jax.ShapeDtypeStruct((bs, 1, width), BF16),
        grid_spec=grid_spec,
        compiler_params=_cparams(("parallel", "arbitrary")),
        name="decode_attn",
    )(page_table, qt, kx, vx, lam_p, sub_g.reshape(1, LANES), _suffix_and_total_matrix(),
      _later_pages_matrix(n_pages, n_heads), *([ck] * pages), *([cv] * pages))
    return out.reshape(bs, width)


def _ffn_kernel(x_ref, g_ref, mod_ref, win_ref, cw_ref, cb_ref, wout_ref, fg_ref,
                y_ref, ff_ref, gs_ref, *, tiles_per_seq, tm, chunk, final):
    i = pl.program_id(0)
    seq = i // tiles_per_seq
    hidden = wout_ref.shape[0]
    halo = SUBLANES

    @pl.when(i % tiles_per_seq == 0)
    def _():
        gs_ref[0:halo, :] = jnp.zeros((halo, hidden), F32)

    x = x_ref[...]
    sh = _mod_row(mod_ref, 3, seq, False)
    sc = _mod_row(mod_ref, 4, seq, False)
    gate = _mod_row(mod_ref, 5, seq, False)
    h = _norm_mod(x, g_ref[...], sc, sh).astype(BF16)
    acc = jnp.zeros(x.shape, F32)
    n_chunks = hidden // chunk

    def up_proj(c):
        lo, up = c * chunk, (c + 1) * chunk
        return (jnp.dot(h, win_ref[:, lo:up], preferred_element_type=F32),
                jnp.dot(h, win_ref[:, hidden + lo:hidden + up], preferred_element_type=F32))

    nxt = up_proj(0)
    for c in range(n_chunks):
        lo, up = c * chunk, (c + 1) * chunk
        g, u = nxt
        if c + 1 < n_chunks:
            nxt = up_proj(c + 1)
        gs_ref[halo:halo + tm, lo:up] = g
        g1 = gs_ref[halo - 1:halo - 1 + tm, lo:up]
        g2 = gs_ref[halo - 2:halo - 2 + tm, lo:up]
        gc = cw_ref[0:1, lo:up] * g2 + cw_ref[1:2, lo:up] * g1 + cw_ref[2:3, lo:up] * g + cb_ref[:, lo:up]
        act = (_silu(gc) * u).astype(BF16)
        acc = acc + jnp.dot(act, wout_ref[lo:up, :], preferred_element_type=F32)
    ff_ref[...] = gs_ref[halo + tm - (FFN_CONV - 1):halo + tm, :]
    gs_ref[0:halo, :] = gs_ref[tm:tm + halo, :]
    y = x + gate * acc
    if final:
        ms = jnp.mean(y * y, axis=-1, keepdims=True)
        y = y * lax.rsqrt(ms + EPS) * fg_ref[...]
    y_ref[...] = y


def _ffn_prompt(x, norm_g, mod, win_bf, conv_w, conv_b, wout_bf, final_g, *, batch, seq_len, final):
    m, d = x.shape
    hidden = wout_bf.shape[0]
    tm = TM_FFN
    tps = seq_len // tm
    return pl.pallas_call(
        functools.partial(_ffn_kernel, tiles_per_seq=tps, tm=tm, chunk=FFN_CHUNK, final=final),
        out_shape=(jax.ShapeDtypeStruct((m, d), F32),
                   jax.ShapeDtypeStruct((batch, FFN_CONV - 1, hidden), F32)),
        grid=(m // tm,),
        in_specs=[
            pl.BlockSpec((tm, d), lambda i: (i, 0)),
            _resident((1, d)),
            _resident(mod.shape),
            _resident(win_bf.shape),
            _resident(conv_w.shape),
            _resident((1, hidden)),
            _resident(wout_bf.shape),
            _resident((1, d)),
        ],
        out_specs=(pl.BlockSpec((tm, d), lambda i: (i, 0)),
                   pl.BlockSpec((None, FFN_CONV - 1, hidden), lambda i: (i // tps, 0, 0))),
        scratch_shapes=[pltpu.VMEM((tm + SUBLANES, hidden), F32)],
        compiler_params=_cparams(("arbitrary",)),
        name="conv_ffn",
    )(x, norm_g.reshape(1, d), mod, win_bf, conv_w, conv_b.reshape(1, hidden), wout_bf,
      final_g.reshape(1, d))


def _ffn_dec_kernel(x_ref, g_ref, mod_ref, wg_ref, wu_ref, past_ref, cw_ref, cb_ref, wout_ref, fg_ref,
                    y_ref, gnew_ref, h_scr, acc_scr, *, final):
    c = pl.program_id(0)

    @pl.when(c == 0)
    def _():
        h = _norm_mod(x_ref[...], g_ref[...], mod_ref[4], mod_ref[3])
        h_scr[...] = h.astype(BF16)
        acc_scr[...] = jnp.zeros_like(acc_scr)

    h = h_scr[...]
    g = jnp.dot(h, wg_ref[...], preferred_element_type=F32)
    u = jnp.dot(h, wu_ref[...], preferred_element_type=F32)
    gnew_ref[...] = g
    gc = cw_ref[0:1, :] * past_ref[0] + cw_ref[1:2, :] * past_ref[1] + cw_ref[2:3, :] * g + cb_ref[...]
    act = (_silu(gc) * u).astype(BF16)
    acc_scr[...] = acc_scr[...] + jnp.dot(act, wout_ref[...], preferred_element_type=F32)

    @pl.when(c == pl.num_programs(0) - 1)
    def _():
        y = x_ref[...] + mod_ref[5] * acc_scr[...]
        if final:
            ms = jnp.mean(y * y, axis=-1, keepdims=True)
            y = y * lax.rsqrt(ms + EPS) * fg_ref[...]
        y_ref[...] = y


def _ffn_decode(x, norm_g, mod, win_bf, past_t, conv_w, conv_b, wout_bf, final_g, *, final):
    m, d = x.shape
    hidden = wout_bf.shape[0]
    chunk = FFN_CHUNK_DEC
    n = hidden // chunk
    return pl.pallas_call(
        functools.partial(_ffn_dec_kernel, final=final),
        out_shape=(jax.ShapeDtypeStruct((m, d), F32), jax.ShapeDtypeStruct((m, hidden), F32)),
        grid=(n,),
        in_specs=[
            pl.BlockSpec((m, d), lambda c: (0, 0)),
            pl.BlockSpec((1, d), lambda c: (0, 0)),
            pl.BlockSpec(mod.shape, lambda c: (0, 0, 0)),
            pl.BlockSpec((d, chunk), lambda c: (0, c)),
            pl.BlockSpec((d, chunk), lambda c: (0, n + c)),
            pl.BlockSpec((FFN_CONV - 1, m, chunk), lambda c: (0, 0, c)),
            pl.BlockSpec((FFN_CONV, chunk), lambda c: (0, c)),
            pl.BlockSpec((1, chunk), lambda c: (0, c)),
            pl.BlockSpec((chunk, d), lambda c: (c, 0)),
            pl.BlockSpec((1, d), lambda c: (0, 0)),
        ],
        out_specs=(pl.BlockSpec((m, d), lambda c: (0, 0)),
                   pl.BlockSpec((m, chunk), lambda c: (0, c))),
        scratch_shapes=[pltpu.VMEM((m, d), BF16), pltpu.VMEM((m, d), F32)],
        compiler_params=_cparams(("arbitrary",)),
        name="conv_ffn_decode",
    )(x, norm_g.reshape(1, d), mod, win_bf, win_bf, past_t, conv_w, conv_b.reshape(1, hidden),
      wout_bf, final_g.reshape(1, d))


def _mixer_kernel(x_ref, g_ref, mod_ref, win_ref, clg_ref, clb_ref, ws_ref, bs_ref, dw_ref, db_ref,
                  dlg_ref, dlb_ref, wout_ref, y_ref, cv_ref, cd_ref, a_scr, sh_scr,
                  *, tiles_per_seq, tm, width):
    i = pl.program_id(0)
    seq = i // tiles_per_seq
    halo = 4 * SUBLANES
    taps = D_CONV

    @pl.when(i % tiles_per_seq == 0)
    def _():
        a_scr[0:halo, :] = jnp.zeros((halo, width), F32)

    x = x_ref[...]
    sh = _mod_row(mod_ref, 0, seq, False)
    sc = _mod_row(mod_ref, 1, seq, False)
    gate = _mod_row(mod_ref, 2, seq, False)
    h = _norm_mod(x, g_ref[...], sc, sh).astype(BF16)

    zu, zv, za, zb = (jnp.dot(h, win_ref[:, n * width:(n + 1) * width], preferred_element_type=F32)
                      for n in range(4))

    u = _gelu_tanh(zu)
    v = _layer_norm(_gelu_tanh(zv), clg_ref[...], clb_ref[...])
    cv_ref[...] = v[tm - CHUNK:tm, :]
    vb = v.astype(BF16)
    gw = width // C_GROUPS
    r = lax.broadcasted_iota(jnp.int32, (CHUNK, CHUNK), 0)
    s = lax.broadcasted_iota(jnp.int32, (CHUNK, CHUNK), 1)
    mixed_rows = []
    for ch in range(tm // CHUNK):
        cols = []
        for grp in range(C_GROUPS):
            wm = jnp.where(s <= r, ws_ref[grp], 0.0).astype(BF16)
            vg = vb[ch * CHUNK:(ch + 1) * CHUNK, grp * gw:(grp + 1) * gw]
            cols.append(jnp.dot(wm, vg, preferred_element_type=F32))
        mixed_rows.append(jnp.concatenate(cols, axis=1) + bs_ref[...])
    s_out = u * jnp.concatenate(mixed_rows, axis=0)

    a = za * jax.nn.sigmoid(zb)
    a_scr[halo:halo + tm, :] = a
    base = halo - (taps - 1)
    dc = jnp.zeros((tm, width), F32) + db_ref[...]
    for s in range(SUBLANES):
        ks = list(range(s, taps, SUBLANES))
        rows = tm + SUBLANES * (len(ks) - 1)
        sh_scr[0:rows, :] = a_scr[base + s:base + s + rows, :]
        for n, k in enumerate(ks):
            dc = dc + dw_ref[k:k + 1, :] * sh_scr[SUBLANES * n:SUBLANES * n + tm, :]
    cd_ref[...] = a_scr[halo + tm - (taps - 1):halo + tm, :]
    a_scr[0:halo, :] = a_scr[tm:tm + halo, :]
    d_out = _silu(_layer_norm(dc, dlg_ref[...], dlb_ref[...]))

    o = jnp.dot(s_out.astype(BF16), wout_ref[0:width, :], preferred_element_type=F32)
    o = o + jnp.dot(d_out.astype(BF16), wout_ref[width:2 * width, :], preferred_element_type=F32)
    y_ref[...] = x + gate * o


def _mixer_prompt(x, norm_g, mod, win_bf, c_ln_g, c_ln_b, ws, bs, dw, db, d_ln_g, d_ln_b, wout_bf,
                  *, batch, seq_len):
    m, d = x.shape
    width = win_bf.shape[1] // 4
    tm = TM_MIX
    tps = seq_len // tm
    gw = width // C_GROUPS
    bs_exp = jnp.repeat(bs.T, gw, axis=1)
    row = lambda a: a.reshape(1, width)
    return pl.pallas_call(
        functools.partial(_mixer_kernel, tiles_per_seq=tps, tm=tm, width=width),
        out_shape=(jax.ShapeDtypeStruct((m, d), F32),
                   jax.ShapeDtypeStruct((batch, CHUNK, width), F32),
                   jax.ShapeDtypeStruct((batch, D_CONV - 1, width), F32)),
        grid=(m // tm,),
        in_specs=[
            pl.BlockSpec((tm, d), lambda i: (i, 0)),
            _resident((1, d)),
            _resident(mod.shape),
            _resident(win_bf.shape),
            _resident((1, width)), _resident((1, width)),
            _resident(ws.shape),
            _resident(bs_exp.shape),
            _resident(dw.shape),
            _resident((1, width)), _resident((1, width)), _resident((1, width)),
            _resident(wout_bf.shape),
        ],
        out_specs=(pl.BlockSpec((tm, d), lambda i: (i, 0)),
                   pl.BlockSpec((None, CHUNK, width), lambda i: (i // tps, 0, 0)),
                   pl.BlockSpec((None, D_CONV - 1, width), lambda i: (i // tps, 0, 0))),
        scratch_shapes=[pltpu.VMEM((tm + 4 * SUBLANES, width), F32),
                        pltpu.VMEM((tm + 3 * SUBLANES, width), F32)],
        compiler_params=_cparams(("arbitrary",)),
        name="mixer",
    )(x, norm_g.reshape(1, d), mod, win_bf, row(c_ln_g), row(c_ln_b), ws, bs_exp, dw, row(db),
      row(d_ln_g), row(d_ln_b), wout_bf)


def _mixer_dec_kernel(x_ref, g_ref, mod_ref, win_ref, clg_ref, clb_ref, w0_ref, b0_ref, past_ref,
                      dw_ref, db_ref, dlg_ref, dlb_ref, wout_ref, y_ref, v_ref, a_ref, *, width):
    x = x_ref[...]
    h = _norm_mod(x, g_ref[...], mod_ref[1], mod_ref[0]).astype(BF16)
    u = _gelu_tanh(jnp.dot(h, win_ref[:, 0:width], preferred_element_type=F32))
    v = _gelu_tanh(jnp.dot(h, win_ref[:, width:2 * width], preferred_element_type=F32))
    v = _layer_norm(v, clg_ref[...], clb_ref[...])
    v_ref[...] = v
    mixed = w0_ref[...] * v + b0_ref[...]
    s_out = u * mixed
    za = jnp.dot(h, win_ref[:, 2 * width:3 * width], preferred_element_type=F32)
    zb = jnp.dot(h, win_ref[:, 3 * width:4 * width], preferred_element_type=F32)
    a = za * jax.nn.sigmoid(zb)
    a_ref[...] = a
    dc = dw_ref[D_CONV - 1:D_CONV, :] * a + db_ref[...]
    for k in range(D_CONV - 1):
        dc = dc + dw_ref[k:k + 1, :] * past_ref[k]
    d_out = _silu(_layer_norm(dc, dlg_ref[...], dlb_ref[...]))
    o = jnp.dot(s_out.astype(BF16), wout_ref[0:width, :], preferred_element_type=F32)
    o = o + jnp.dot(d_out.astype(BF16), wout_ref[width:2 * width, :], preferred_element_type=F32)
    y_ref[...] = x + mod_ref[2] * o


def _mixer_decode(x, norm_g, mod, win_bf, c_ln_g, c_ln_b, ws, bs, past_t, dw, db, d_ln_g, d_ln_b, wout_bf):
    m, d = x.shape
    width = win_bf.shape[1] // 4
    gw = width // C_GROUPS
    row = lambda a: a.reshape(1, width)
    w0 = jnp.repeat(ws[:, 0, 0], gw).reshape(1, width)
    b0 = jnp.repeat(bs[:, 0], gw).reshape(1, width)
    args = (x, norm_g.reshape(1, d), mod, win_bf, row(c_ln_g), row(c_ln_b), w0, b0, past_t, dw, row(db),
            row(d_ln_g), row(d_ln_b), wout_bf)
    full = lambda a: pl.BlockSpec(a.shape, lambda i, nd=a.ndim: (0,) * nd)
    return pl.pallas_call(
        functools.partial(_mixer_dec_kernel, width=width),
        out_shape=(jax.ShapeDtypeStruct((m, d), F32),
                   jax.ShapeDtypeStruct((m, width), F32),
                   jax.ShapeDtypeStruct((m, width), F32)),
        grid=(1,),
        in_specs=[full(a) for a in args],
        out_specs=(pl.BlockSpec((m, d), lambda i: (0, 0)),
                   pl.BlockSpec((m, width), lambda i: (0, 0)),
                   pl.BlockSpec((m, width), lambda i: (0, 0))),
        compiler_params=_cparams(("arbitrary",)),
        name="mixer_decode",
    )(*args)


def kernel(x_prompt, x_sample, c_prompt, c_sample, cache_k, cache_v, state_conv_d, state_ffn_conv,
           page_table, norm_g, ada_w, ada_b, att_w_in, att_lam, att_sub_g, att_w_out, mix_w_in,
           c_ln_g, c_ln_b, c_ws, c_bs, d_conv_w, d_conv_b, d_ln_g, d_ln_b, mix_w_out,
           ffn_w_in, ffn_conv_w, ffn_conv_b, ffn_w_out, final_g):
    batch, seq_len, d = x_prompt.shape
    bs = x_sample.shape[0]
    depth = ada_w.shape[0]
    att_width = att_w_out.shape[1]
    n_heads = att_width // HEAD_DIM
    hidden = ffn_w_out.shape[1]
    assert x_sample.shape[1] == 1 and seq_len % TQ == 0 and seq_len % TM_PROJ == 0
    assert bs % (2 * SUBLANES) == 0 and page_table.shape[1] % PAGES_PER_STEP == 0

    rows = bs + batch
    pad = (-rows) % (2 * SUBLANES)
    c_all = jnp.concatenate([c_sample, c_prompt, jnp.zeros((pad, d), F32)], axis=0)
    mod = _ada_mod(c_all, ada_w, ada_b)
    mod_s = mod[:, :, 0:bs]
    mod_p = mod[:, :, bs:bs + batch]

    xp = x_prompt.reshape(batch * seq_len, d)
    xs = x_sample.reshape(bs, d)
    k_s, v_s, cv_p, cv_s, cd_p, cd_s, ff_p, ff_s = ([] for _ in range(8))
    kt_p = vt_p = None

    for i in range(depth):
        j = i // 2
        if i % 2 == 0:
            lam_init = 0.8 - 0.6 * math.exp(-0.3 * i)
            w_in = att_w_in[j].astype(BF16)
            w_out = att_w_out[j].astype(BF16)
            n_sb = N_SB_HEADS * HEAD_DIM
            wq = w_in[:, 0:att_width]
            wkt = w_in[:, att_width:2 * att_width].T
            wvt = w_in[:, 2 * att_width:3 * att_width].T
            q, kt_p, vt_p, ktb, vtb = _qkv_prompt(xp, norm_g[i, 0], mod_p[i], wq, wkt, wvt, kt_p, vt_p,
                                                  batch=batch, seq_len=seq_len, tm=TQ)
            o_sb = _attn_sb(q, ktb, vtb, batch=batch, seq_len=seq_len, tq=TQ)
            o_d = _attn_diff(q, ktb, vtb, att_lam[j], att_sub_g[j], batch=batch, seq_len=seq_len, tq=TQ,
                             lam_init=lam_init)
            xp = _out_proj(xp, o_sb, o_d, mod_p[i], w_out, tm=TM_PROJ,
                           tiles_per_seq=seq_len // TM_PROJ, per_row=False)
            q, k, v = _qkv_decode(xs, norm_g[i, 0], mod_s[i], w_in)
            o = _decode_attn(q, k, v, cache_k, cache_v, page_table, att_lam[j], att_sub_g[j],
                             layer=j, lam_init=lam_init)
            xs = _out_proj(xs, o[:, 0:n_sb], o[:, n_sb:], mod_s[i], w_out, tm=bs, tiles_per_seq=1,
                           per_row=True)
            k_s.append(k.reshape(bs, 1, n_heads, HEAD_DIM))
            v_s.append(v.reshape(bs, 1, n_heads, HEAD_DIM))
        else:
            w_in = mix_w_in[j].astype(BF16)
            w_out = mix_w_out[j].astype(BF16)
            xp, cv, cd = _mixer_prompt(xp, norm_g[i, 0], mod_p[i], w_in, c_ln_g[j], c_ln_b[j], c_ws[j],
                                       c_bs[j], d_conv_w[j], d_conv_b[j], d_ln_g[j], d_ln_b[j], w_out,
                                       batch=batch, seq_len=seq_len)
            cv_p.append(cv)
            cd_p.append(cd)
            past = state_conv_d[j]
            xs, v_new, a_new = _mixer_decode(xs, norm_g[i, 0], mod_s[i], w_in, c_ln_g[j], c_ln_b[j],
                                             c_ws[j], c_bs[j], past.transpose(1, 0, 2), d_conv_w[j],
                                             d_conv_b[j], d_ln_g[j], d_ln_b[j], w_out)
            cv_s.append(v_new[:, None, :])
            cd_s.append(jnp.concatenate([past[:, 1:], a_new[:, None, :]], axis=1))
        final = i == depth - 1
        w_in = ffn_w_in[i].astype(BF16)
        w_out = ffn_w_out[i].astype(BF16)
        xp, ff = _ffn_prompt(xp, norm_g[i, 1], mod_p[i], w_in, ffn_conv_w[i], ffn_conv_b[i], w_out,
                             final_g, batch=batch, seq_len=seq_len, final=final)
        ff_p.append(ff)
        past = state_ffn_conv[i]
        xs, g_new = _ffn_decode(xs, norm_g[i, 1], mod_s[i], w_in, past.transpose(1, 0, 2), ffn_conv_w[i],
                                ffn_conv_b[i], w_out, final_g, final=final)
        ff_s.append(jnp.concatenate([past[:, 1:], g_new[:, None, :]], axis=1))

    def to_cache_layout(t):
        return t.reshape(t.shape[0], batch, n_heads, HEAD_DIM, seq_len).transpose(0, 1, 4, 2, 3)

    return (xp.reshape(batch, seq_len, d), xs.reshape(bs, 1, d),
            to_cache_layout(kt_p), to_cache_layout(vt_p), jnp.stack(k_s), jnp.stack(v_s),
            jnp.stack(cv_p), jnp.stack(cv_s), jnp.stack(cd_p), jnp.stack(cd_s),
            jnp.stack(ff_p), jnp.stack(ff_s))
```

```python
import functools
import math

import jax
import jax.numpy as jnp
from jax import lax
from jax.experimental import pallas as pl
from jax.experimental.pallas import tpu as pltpu

F32 = jnp.float32
BF16 = jnp.bfloat16

HEAD_DIM = 64
N_SB_HEADS = 8
N_DIFF_HEADS = 4
N_MOD = 6
C_GROUPS = 4
CHUNK = 128
D_CONV = 31
FFN_CONV = 3
EPS = 1e-6
NEG_INF = -1e30

LANES = 128
SUBLANES = 8
VMEM_LIMIT = 56 * 1024 * 1024

TM_PROJ = 512
TM_FFN = 256
TM_MIX = 256
TQ = 256
PAGES_PER_STEP = 16
FFN_CHUNK = 256
FFN_CHUNK_DEC = 1408


def _cparams(sem, vmem=VMEM_LIMIT):
    return pltpu.CompilerParams(dimension_semantics=sem, vmem_limit_bytes=vmem)


def _resident(shape):
    nd = len(shape)
    return pl.BlockSpec(shape, lambda *_: (0,) * nd, pipeline_mode=pl.Buffered(1))


def _norm_mod(x, g, sc, sh):
    ms = jnp.mean(x * x, axis=-1, keepdims=True)
    y = x * lax.rsqrt(ms + EPS) * g
    return y * (1.0 + sc) + sh


def _mod_row(mod_ref, slot, seq, per_row):
    if per_row:
        return mod_ref[slot]
    return mod_ref[slot, pl.ds(seq, 1), :]


def _layer_norm(x, g, b):
    mu = jnp.mean(x, axis=-1, keepdims=True)
    xc = x - mu
    var = jnp.mean(xc * xc, axis=-1, keepdims=True)
    return xc * lax.rsqrt(var + EPS) * g + b


def _gelu_tanh(x):
    c = math.sqrt(2.0 / math.pi)
    return x * (0.5 * (1.0 + jnp.tanh(c * (x + 0.044715 * (x * x * x)))))


def _silu(x):
    return x * jax.nn.sigmoid(x)


def _softplus(z):
    return jnp.maximum(z, 0.0) + jnp.log(1.0 + jnp.exp(-jnp.abs(z)))


def _split_bf16(x):
    hi = x.astype(BF16)
    lo = (x - hi.astype(F32)).astype(BF16)
    return hi, lo


def _ada_kernel(c_ref, w_ref, b_ref, o_ref):
    c = c_ref[...]
    s = _silu(c).astype(BF16)
    w = w_ref[...].astype(BF16)
    o_ref[...] = jnp.dot(s, w, preferred_element_type=F32) + b_ref[...]


def _ada_mod(c_all, ada_w, ada_b):
    depth, d, _ = ada_w.shape
    rows = c_all.shape[0]
    return pl.pallas_call(
        _ada_kernel,
        out_shape=jax.ShapeDtypeStruct((depth, N_MOD, rows, d), F32),
        grid=(depth, N_MOD),
        in_specs=[
            pl.BlockSpec((rows, d), lambda i, j: (0, 0)),
            pl.BlockSpec((None, d, d), lambda i, j: (i, 0, j)),
            pl.BlockSpec((None, None, 1, d), lambda i, j: (i, j, 0, 0)),
        ],
        out_specs=pl.BlockSpec((None, None, rows, d), lambda i, j: (i, j, 0, 0)),
        compiler_params=_cparams(("parallel", "parallel")),
        name="ada_mod",
    )(c_all, ada_w, ada_b.reshape(depth, N_MOD, 1, d))


def _qkv_dec_kernel(x_ref, g_ref, mod_ref, w_ref, q_ref, k_ref, v_ref, *, width):
    h = _norm_mod(x_ref[...], g_ref[...], mod_ref[1], mod_ref[0]).astype(BF16)
    q = jnp.dot(h, w_ref[:, 0:width], preferred_element_type=F32)
    q_ref[...] = (q * (HEAD_DIM ** -0.5)).astype(BF16)
    k_ref[...] = jnp.dot(h, w_ref[:, width:2 * width], preferred_element_type=F32)
    v_ref[...] = jnp.dot(h, w_ref[:, 2 * width:3 * width], preferred_element_type=F32)


def _qkv_decode(x, norm_g, mod, w_bf):
    m, d = x.shape
    width = w_bf.shape[1] // 3
    tile = lambda dt: jax.ShapeDtypeStruct((m, width), dt)
    full = lambda shape: pl.BlockSpec(shape, lambda i: (0,) * len(shape))
    return pl.pallas_call(
        functools.partial(_qkv_dec_kernel, width=width),
        out_shape=(tile(BF16), tile(F32), tile(F32)),
        grid=(1,),
        in_specs=[full((m, d)), full((1, d)), full(mod.shape), full(w_bf.shape)],
        out_specs=(full((m, width)),) * 3,
        compiler_params=_cparams(("arbitrary",)),
        name="qkv_decode",
    )(x, norm_g.reshape(1, d), mod, w_bf)


def _qkv_t_kernel(x_ref, g_ref, mod_ref, wq_ref, wkt_ref, wvt_ref, *rest, tiles_per_seq, n_prev):
    if n_prev:
        pk_ref, pv_ref = rest[0:2]
        rest = rest[2:]
    q_ref, kt_ref, vt_ref, ktb_ref, vtb_ref = rest
    seq = pl.program_id(0) // tiles_per_seq
    sh = _mod_row(mod_ref, 0, seq, False)
    sc = _mod_row(mod_ref, 1, seq, False)
    h = _norm_mod(x_ref[...], g_ref[...], sc, sh).astype(BF16)
    q = jnp.dot(h, wq_ref[...], preferred_element_type=F32)
    q_ref[...] = (q * (HEAD_DIM ** -0.5)).astype(BF16)
    nt = (((1,), (1,)), ((), ()))
    kt = lax.dot_general(wkt_ref[...], h, nt, preferred_element_type=F32)
    vt = lax.dot_general(wvt_ref[...], h, nt, preferred_element_type=F32)
    if n_prev:
        kt_ref[0:n_prev] = pk_ref[...]
        vt_ref[0:n_prev] = pv_ref[...]
    kt_ref[n_prev] = kt
    vt_ref[n_prev] = vt
    ktb_ref[...] = kt.astype(BF16)
    vtb_ref[...] = vt.astype(BF16)


def _qkv_prompt(x, norm_g, mod, wq, wkt, wvt, prev_k, prev_v, *, batch, seq_len, tm):
    m, d = x.shape
    width = wq.shape[1]
    tps = seq_len // tm
    n_prev = 0 if prev_k is None else prev_k.shape[0]
    slab = lambda n: pl.BlockSpec((n, None, width, tm), lambda i: (0, i // tps, 0, i % tps))
    blocked = pl.BlockSpec((None, None, width, tm), lambda i: (i // tps, i % tps, 0, 0))
    in_specs = [
        pl.BlockSpec((tm, d), lambda i: (i, 0)),
        _resident((1, d)),
        _resident(mod.shape),
        _resident(wq.shape), _resident(wkt.shape), _resident(wvt.shape),
    ]
    args = [x, norm_g.reshape(1, d), mod, wq, wkt, wvt]
    if n_prev:
        in_specs += [slab(n_prev), slab(n_prev)]
        args += [prev_k, prev_v]
    stacked = jax.ShapeDtypeStruct((n_prev + 1, batch, width, seq_len), F32)
    blk = jax.ShapeDtypeStruct((batch, tps, width, tm), BF16)
    return pl.pallas_call(
        functools.partial(_qkv_t_kernel, tiles_per_seq=tps, n_prev=n_prev),
        out_shape=(jax.ShapeDtypeStruct((m, width), BF16), stacked, stacked, blk, blk),
        grid=(m // tm,),
        in_specs=in_specs,
        out_specs=(pl.BlockSpec((tm, width), lambda i: (i, 0)), slab(n_prev + 1), slab(n_prev + 1),
                   blocked, blocked),
        compiler_params=_cparams(("parallel",)),
        name="qkv_prompt",
    )(*args)


def _oproj_kernel(x_ref, oa_ref, ob_ref, mod_ref, w_ref, y_ref, *, tiles_per_seq, per_row, split):
    seq = pl.program_id(0) // tiles_per_seq
    gate = _mod_row(mod_ref, 2, seq, per_row)
    o = jnp.dot(oa_ref[...], w_ref[0:split, :], preferred_element_type=F32)
    o = o + jnp.dot(ob_ref[...], w_ref[split:, :], preferred_element_type=F32)
    y_ref[...] = x_ref[...] + gate * o


def _out_proj(x, oa, ob, mod, w_bf, *, tm, tiles_per_seq, per_row):
    m, d = x.shape
    split = oa.shape[1]
    return pl.pallas_call(
        functools.partial(_oproj_kernel, tiles_per_seq=tiles_per_seq, per_row=per_row, split=split),
        out_shape=jax.ShapeDtypeStruct((m, d), F32),
        grid=(m // tm,),
        in_specs=[
            pl.BlockSpec((tm, d), lambda i: (i, 0)),
            pl.BlockSpec((tm, split), lambda i: (i, 0)),
            pl.BlockSpec((tm, ob.shape[1]), lambda i: (i, 0)),
            _resident(mod.shape),
            _resident(w_bf.shape),
        ],
        out_specs=pl.BlockSpec((tm, d), lambda i: (i, 0)),
        compiler_params=_cparams(("parallel",)),
        name="out_proj",
    )(x, oa, ob, mod, w_bf)


_NT = (((1,), (1,)), ((), ()))


def _sb_kernel(q_ref, kt_ref, vt_ref, uu_ref, o_ref, *, tq):
    _sb_body(pl.program_id(2), q_ref, kt_ref, vt_ref, uu_ref, o_ref, tq)


def _sb_body(i, q_ref, kt_ref, vt_ref, uu_ref, o_ref, tq):
    q = q_ref[...]
    lane = lax.broadcasted_iota(jnp.int32, (1, LANES), 1)
    row = lax.broadcasted_iota(jnp.int32, (tq, tq), 0)
    col = lax.broadcasted_iota(jnp.int32, (tq, tq), 1)
    strict = col < row
    uu = uu_ref[...]
    zero_q = jnp.zeros_like(q)
    qs = (jnp.where(lane < HEAD_DIM, q, zero_q), jnp.where(lane >= HEAD_DIM, q, zero_q))

    def blocks(js, state, diag):
        kbs = [kt_ref[j] for j in js]
        vbs = [vt_ref[j] for j in js]
        chains = [(n, h) for n in range(len(js)) for h in range(2)]
        z = {c: jnp.dot(qs[c[1]], kbs[c[0]], preferred_element_type=F32) for c in chains}
        sp = {c: _softplus(z[c]) for c in chains}
        lk = {c: jnp.where(strict, sp[c], 0.0) if (diag and c[0] == 0) else sp[c] for c in chains}
        cs = {c: jnp.dot(jnp.concatenate(_split_bf16(lk[c]), axis=1), uu, preferred_element_type=F32)
              for c in chains}
        tot = {c: jnp.sum(lk[c], axis=1, keepdims=True) for c in chains}
        new = []
        for h in range(2):
            carry, acc = state[2 * h], state[2 * h + 1]
            for n in range(len(js)):
                c = (n, h)
                a = jnp.exp(z[c] - sp[c] - cs[c] - carry)
                if diag and n == 0:
                    a = jnp.where(strict, a, 0.0)
                acc = acc + lax.dot_general(a.astype(BF16), vbs[n], _NT, preferred_element_type=F32)
                carry = carry + tot[c]
            new += [carry, acc]
        return tuple(new)

    c0 = jnp.zeros((tq, 1), F32)
    a0 = jnp.zeros((tq, LANES), F32)
    state = (c0, a0, c0, a0)
    has_prev = jnp.minimum(i, 1)
    state = lax.fori_loop(0, 1 - has_prev, lambda t, st: blocks([i], st, True), state)
    state = lax.fori_loop(0, has_prev, lambda t, st: blocks([i, i - 1], st, True), state)
    rest = jnp.maximum(i - 1, 0)
    state = lax.fori_loop(0, rest // 2, lambda t, st: blocks([i - 2 - 2 * t, i - 3 - 2 * t], st, False), state)
    state = lax.fori_loop(0, rest % 2, lambda t, st: blocks([0], st, False), state)
    o_ref[...] = jnp.where(lane < HEAD_DIM, state[1], state[3]).astype(o_ref.dtype)


def _suffix_matrix(n):
    j = lax.broadcasted_iota(jnp.int32, (n, n), 0)
    s = lax.broadcasted_iota(jnp.int32, (n, n), 1)
    u = (j > s).astype(BF16)
    return jnp.concatenate([u, u], axis=0)


def _attn_sb(q_bf, ktb, vtb, *, batch, seq_len, tq):
    m = q_bf.shape[0]
    nq = seq_len // tq
    n_pairs = N_SB_HEADS * HEAD_DIM // LANES
    kv_spec = pl.BlockSpec((None, nq, LANES, tq), lambda b, p, i: (b, 0, p, 0))
    return pl.pallas_call(
        functools.partial(_sb_kernel, tq=tq),
        out_shape=jax.ShapeDtypeStruct((m, n_pairs * LANES), BF16),
        grid=(batch, n_pairs, nq),
        in_specs=[
            pl.BlockSpec((tq, LANES), lambda b, p, i: (b * nq + i, p)),
            kv_spec, kv_spec,
            _resident((2 * tq, tq)),
        ],
        out_specs=pl.BlockSpec((tq, LANES), lambda b, p, i: (b * nq + i, p)),
        compiler_params=_cparams(("parallel", "parallel", "parallel")),
        name="attn_sb",
    )(q_bf, ktb, vtb, _suffix_matrix(tq))


def _lam_value(lam_ref, lam_init):
    lp = lam_ref[...]
    t1 = jnp.sum(lp[0:1] * lp[1:2], axis=1, keepdims=True)
    t2 = jnp.sum(lp[2:3] * lp[3:4], axis=1, keepdims=True)
    return jnp.exp(t1) - jnp.exp(t2) + lam_init


def _diff_kernel(lam_ref, g_ref, q_ref, kt_ref, vt_ref, o_ref, vext_ref, *, tq, lam_init):
    i = pl.program_id(2)

    @pl.when(i == 0)
    def _():
        vext_ref[:, 0:LANES, :] = vt_ref[...]
        vext_ref[:, LANES:2 * LANES, :] = jnp.ones(vt_ref.shape, BF16)

    q = q_ref[...]
    lane = lax.broadcasted_iota(jnp.int32, (1, LANES), 1)
    row = lax.broadcasted_iota(jnp.int32, (tq, tq), 0)
    col = lax.broadcasted_iota(jnp.int32, (tq, tq), 1)
    causal = col <= row
    zero_q = jnp.zeros_like(q)
    qs = (jnp.where(lane < HEAD_DIM, q, zero_q), jnp.where(lane >= HEAD_DIM, q, zero_q))

    def blocks(js, state, diag):
        kbs = [kt_ref[j] for j in js]
        ves = [vext_ref[j] for j in js]
        s = {(n, c): jnp.dot(qs[c], kbs[n], preferred_element_type=F32)
             for n in range(len(js)) for c in range(2)}
        new = []
        for c in range(2):
            m_old, acc = state[2 * c], state[2 * c + 1]
            sc = [jnp.where(causal, s[(n, c)], NEG_INF) if (diag and n == 0) else s[(n, c)]
                  for n in range(len(js))]
            m_new = m_old
            for n in range(len(js)):
                m_new = jnp.maximum(m_new, jnp.max(sc[n], axis=1, keepdims=True))
            acc = jnp.exp(m_old - m_new) * acc
            for n in range(len(js)):
                p = jnp.exp(sc[n] - m_new)
                acc = acc + lax.dot_general(p.astype(BF16), ves[n], _NT, preferred_element_type=F32)
            new += [m_new, acc]
        return tuple(new)

    m0 = jnp.full((tq, 1), NEG_INF, F32)
    a0 = jnp.zeros((tq, 2 * LANES), F32)
    state = (m0, a0, m0, a0)
    has_prev = jnp.minimum(i, 1)
    state = lax.fori_loop(0, 1 - has_prev, lambda t, st: blocks([i], st, True), state)
    state = lax.fori_loop(0, has_prev, lambda t, st: blocks([i, i - 1], st, True), state)
    rest = jnp.maximum(i - 1, 0)
    state = lax.fori_loop(0, rest // 2, lambda t, st: blocks([i - 2 - 2 * t, i - 3 - 2 * t], st, False), state)
    state = lax.fori_loop(0, rest % 2, lambda t, st: blocks([0], st, False), state)

    lam = _lam_value(lam_ref, lam_init)
    o1 = state[1][:, 0:LANES] / state[1][:, LANES:2 * LANES]
    o2 = state[3][:, 0:LANES] / state[3][:, LANES:2 * LANES]
    o = o1 - lam * o2
    ms = jnp.mean(o * o, axis=1, keepdims=True)
    o = o * lax.rsqrt(ms + EPS) * g_ref[...] * (1.0 - lam_init)
    o_ref[...] = o.astype(o_ref.dtype)


def _attn_diff(q_bf, ktb, vtb, lam_p, sub_g, *, batch, seq_len, tq, lam_init):
    m = q_bf.shape[0]
    nq = seq_len // tq
    off = N_SB_HEADS * HEAD_DIM // LANES
    kv_spec = pl.BlockSpec((None, nq, LANES, tq), lambda b, p, i: (b, 0, off + p, 0))
    return pl.pallas_call(
        functools.partial(_diff_kernel, tq=tq, lam_init=lam_init),
        out_shape=jax.ShapeDtypeStruct((m, N_DIFF_HEADS * LANES), BF16),
        grid=(batch, N_DIFF_HEADS, nq),
        in_specs=[
            _resident(lam_p.shape),
            _resident((1, LANES)),
            pl.BlockSpec((tq, LANES), lambda b, p, i: (b * nq + i, off + p)),
            kv_spec, kv_spec,
        ],
        out_specs=pl.BlockSpec((tq, LANES), lambda b, p, i: (b * nq + i, p)),
        scratch_shapes=[pltpu.VMEM((nq, 2 * LANES, tq), BF16)],
        compiler_params=_cparams(("parallel", "parallel", "arbitrary")),
        name="attn_diff",
    )(lam_p, sub_g.reshape(1, LANES), q_bf, ktb, vtb)


def _dec_attn_kernel(pt_ref, *refs, n_steps, pages, n_heads, lam_init):
    del pt_ref
    _dec_body(pl.program_id(0), pl.program_id(1), refs[0:7 + 2 * pages], refs[7 + 2 * pages],
              refs[8 + 2 * pages:], n_steps=n_steps, pages=pages, n_heads=n_heads, lam_init=lam_init)


def _dec_body(seq, t, in_refs, o_ref, scratch, *, n_steps, pages, n_heads, lam_init):
    qt_ref, kx_ref, vx_ref, lam_ref, g_ref, uo_ref, msuf_ref = in_refs[0:7]
    k_refs = in_refs[7:7 + pages]
    v_refs = in_refs[7 + pages:7 + 2 * pages]
    s_scr, w_scr, x_scr, acc_scr, qb_scr = scratch
    n_pages = n_steps * pages
    n_rows = n_pages * n_heads
    lane = lax.broadcasted_iota(jnp.int32, (1, LANES), 1)
    rid = lax.broadcasted_iota(jnp.int32, (n_heads, 1), 0)
    is_sb = rid < N_SB_HEADS

    def value_row(wp, h):
        if h < N_SB_HEADS:
            return wp[h:h + 1, :]
        first = N_SB_HEADS + 2 * ((h - N_SB_HEADS) // 2)
        return wp[first:first + 1, :] + wp[first + 1:first + 2, :]

    def q_head(h):
        return qb_scr[h * HEAD_DIM:(h + 1) * HEAD_DIM, :]

    @pl.when(t == 0)
    def _():
        acc_scr[...] = jnp.zeros_like(acc_scr)
        src = lax.broadcasted_iota(jnp.int32, (LANES, LANES), 0)
        pick = jnp.where(src == seq, 1.0, 0.0).astype(BF16)
        qb_scr[...] = jnp.dot(qt_ref[...], pick, preferred_element_type=F32)

    @pl.when(t < n_steps)
    def _():
        for h in range(n_heads):
            qh = q_head(h)
            for r in range(pages):
                row = jnp.sum(k_refs[r][h] * qh, axis=0, keepdims=True)
                s_scr[t * pages + r, pl.ds(h, 1), :] = row

    @pl.when(t == n_steps - 1)
    def _():
        for h in range(n_heads):
            x_scr[pl.ds(h, 1), :] = jnp.sum(kx_ref[h] * q_head(h), axis=0, keepdims=True)
        s3 = s_scr[...]
        s = s3.reshape(n_rows, LANES)
        sp = _softplus(s)
        hi, lo = _split_bf16(sp)
        r1 = jnp.dot(jnp.concatenate([hi, lo], axis=1), uo_ref[...], preferred_element_type=F32)
        cs = r1[:, 0:LANES]
        tot = r1[:, LANES:2 * LANES]
        t1 = tot.astype(BF16)
        rem = tot - t1.astype(F32)
        t2, t3 = _split_bf16(rem)
        c3 = jnp.dot(msuf_ref[...], jnp.concatenate([t1, t2, t3], axis=1), preferred_element_type=F32)
        carry = c3[:, 0:LANES] + c3[:, LANES:2 * LANES] + c3[:, 2 * LANES:3 * LANES]
        a3 = jnp.exp(s - sp - cs - carry).reshape(n_pages, n_heads, LANES)
        sx = jnp.where(lane == seq, x_scr[...], NEG_INF)
        m = jnp.max(jnp.maximum(jnp.max(s3, axis=0), sx), axis=1, keepdims=True)
        p3 = jnp.exp(s3 - m[None])
        px = jnp.exp(sx - m)
        l = jnp.sum(jnp.sum(p3, axis=0) + px, axis=1, keepdims=True)
        lam = _lam_value(lam_ref, lam_init)
        coef = jnp.where(lax.bitwise_and(rid, 1) == 0, 1.0, -lam) / l
        w_scr[...] = jnp.where(is_sb[None], a3, p3 * coef[None])
        x_scr[...] = jnp.where(is_sb, 0.0, px * coef)

    @pl.when(t >= n_steps)
    def _():
        blk = t - n_steps
        wps = [w_scr[blk * pages + r] for r in range(pages)]
        for h in range(n_heads):
            acc = acc_scr[h]
            for r in range(pages):
                acc = acc + value_row(wps[r], h) * v_refs[r][h]
            acc_scr[h] = acc

    @pl.when(t == 2 * n_steps - 1)
    def _():
        wx = x_scr[...]
        for h in range(n_heads):
            acc_scr[h] = acc_scr[h] + value_row(wx, h) * vx_ref[h]
        hi, lo = _split_bf16(acc_scr[...].reshape(n_heads * HEAD_DIM, LANES))
        ones = jnp.ones((SUBLANES, LANES), BF16)
        nt = (((1,), (1,)), ((), ()))
        o = lax.dot_general(ones, hi, nt, preferred_element_type=F32)
        o = (o + lax.dot_general(ones, lo, nt, preferred_element_type=F32))[0:1, :]
        n_sb = N_SB_HEADS * HEAD_DIM
        o_ref[:, 0:n_sb] = o[:, 0:n_sb].astype(o_ref.dtype)
        for hd in range(N_DIFF_HEADS):
            od = o[:, n_sb + hd * LANES:n_sb + (hd + 1) * LANES]
            ms = jnp.mean(od * od, axis=1, keepdims=True)
            od = od * lax.rsqrt(ms + EPS) * g_ref[...] * (1.0 - lam_init)
            o_ref[:, n_sb + hd * LANES:n_sb + (hd + 1) * LANES] = od.astype(o_ref.dtype)


def _suffix_and_total_matrix():
    j = lax.broadcasted_iota(jnp.int32, (LANES, LANES), 0)
    s = lax.broadcasted_iota(jnp.int32, (LANES, LANES), 1)
    u1 = jnp.concatenate([(j > s).astype(BF16), jnp.ones((LANES, LANES), BF16)], axis=1)
    return jnp.concatenate([u1, u1], axis=0)


def _later_pages_matrix(n_pages, n_heads):
    n = n_pages * n_heads
    a = lax.broadcasted_iota(jnp.int32, (n, n), 0)
    b = lax.broadcasted_iota(jnp.int32, (n, n), 1)
    same_head = (a % n_heads) == (b % n_heads)
    return jnp.logical_and(same_head, b // n_heads > a // n_heads).astype(BF16)


def _decode_operands(q_bf, k_new, v_new, cache_k, cache_v, page_table, lam_p, sub_g, *, layer, seq_step):
    bs, width = q_bf.shape
    page, n_heads = cache_k.shape[2], cache_k.shape[3]
    assert page == LANES and cache_k.shape[4] == HEAD_DIM and bs <= LANES
    n_pages = page_table.shape[1]
    pages = PAGES_PER_STEP
    n_steps = n_pages // pages
    ck = cache_k.transpose(0, 1, 3, 4, 2)
    cv = cache_v.transpose(0, 1, 3, 4, 2)
    lanes = lambda a: jnp.pad(a.T, ((0, 0), (0, LANES - bs)))
    qt = lanes(q_bf)
    kx = lanes(k_new).reshape(n_heads, HEAD_DIM, LANES)
    vx = lanes(v_new).reshape(n_heads, HEAD_DIM, LANES)

    def k_map(r):
        def index(*ids):
            seq, t = seq_step(*ids[:-1])
            return (layer, ids[-1][seq, jnp.minimum(t, n_steps - 1) * pages + r], 0, 0, 0)
        return index

    def v_map(r):
        def index(*ids):
            seq, t = seq_step(*ids[:-1])
            return (layer, ids[-1][seq, jnp.maximum(t - n_steps, 0) * pages + r], 0, 0, 0)
        return index

    const = lambda shape: pl.BlockSpec(shape, lambda *ids: (0,) * len(shape))
    page_block = (None, None, n_heads, HEAD_DIM, LANES)
    n_rows = n_pages * n_heads
    in_specs = [const(qt.shape), const(kx.shape), const(vx.shape),
                const(lam_p.shape), const((1, LANES)), const((2 * LANES, 2 * LANES)),
                const((n_rows, n_rows))]
    in_specs += [pl.BlockSpec(page_block, k_map(r)) for r in range(pages)]
    in_specs += [pl.BlockSpec(page_block, v_map(r)) for r in range(pages)]
    args = [qt, kx, vx, lam_p, sub_g.reshape(1, LANES), _suffix_and_total_matrix(),
            _later_pages_matrix(n_pages, n_heads)] + [ck] * pages + [cv] * pages
    out_spec = pl.BlockSpec((None, 1, width), lambda *ids: (seq_step(*ids[:-1])[0], 0, 0))
    scratch = [
        pltpu.VMEM((n_pages, n_heads, LANES), F32),
        pltpu.VMEM((n_pages, n_heads, LANES), F32),
        pltpu.VMEM((n_heads, LANES), F32),
        pltpu.VMEM((n_heads, HEAD_DIM, LANES), F32),
        pltpu.VMEM((width, LANES), F32),
    ]
    static = dict(n_steps=n_steps, pages=pages, n_heads=n_heads)
    return args, in_specs, out_spec, scratch, static


def _decode_attn(q_bf, k_new, v_new, cache_k, cache_v, page_table, lam_p, sub_g, *, layer, lam_init):
    bs, width = q_bf.shape
    args, in_specs, out_spec, scratch, static = _decode_operands(
        q_bf, k_new, v_new, cache_k, cache_v, page_table, lam_p, sub_g, layer=layer,
        seq_step=lambda b, t: (b, t))
    grid_spec = pltpu.PrefetchScalarGridSpec(
        num_scalar_prefetch=1, grid=(bs, 2 * static["n_steps"]),
        in_specs=in_specs, out_specs=out_spec, scratch_shapes=scratch)
    out = pl.pallas_call(
        functools.partial(_dec_attn_kernel, lam_init=lam_init, **static),
        out_shape=jax.ShapeDtypeStruct((bs, 1, width), BF16),
        grid_spec=grid_spec,
        compiler_params=_cparams(("parallel", "arbitrary")),
        name="decode_attn",
    )(page_table, *args)
    return out.reshape(bs, width)


def _sb_dec_kernel(pt_ref, q_ref, kt_ref, vt_ref, uu_ref, *rest, tq, n_pairs, nq, n_dec, n_steps, pages,
                   n_heads, lam_init):
    del pt_ref
    dec_in = rest[0:n_dec]
    o_sb_ref, o_dec_ref = rest[n_dec:n_dec + 2]
    scratch = rest[n_dec + 2:]
    i = pl.program_id(2)
    step = (pl.program_id(0) * n_pairs + pl.program_id(1)) * nq + i
    _sb_body(i, q_ref, kt_ref, vt_ref, uu_ref, o_sb_ref, tq)
    _dec_body(step // (2 * n_steps), step % (2 * n_steps), dec_in, o_dec_ref, scratch,
              n_steps=n_steps, pages=pages, n_heads=n_heads, lam_init=lam_init)


def _fused_steps_match(batch, seq_len, tq, bs, n_pages):
    n_pairs = N_SB_HEADS * HEAD_DIM // LANES
    return batch * n_pairs * (seq_len // tq) == bs * 2 * (n_pages // PAGES_PER_STEP)


def _attn_sb_decode(q_p, ktb, vtb, q_s, k_new, v_new, cache_k, cache_v, page_table, lam_p, sub_g,
                    *, batch, seq_len, tq, layer, lam_init):
    m = q_p.shape[0]
    bs, width = q_s.shape
    nq = seq_len // tq
    n_pairs = N_SB_HEADS * HEAD_DIM // LANES

    dec_steps = 2 * (page_table.shape[1] // PAGES_PER_STEP)

    def seq_step(b, p, i):
        step = (b * n_pairs + p) * nq + i
        return step // dec_steps, step % dec_steps

    args, dec_specs, dec_out, scratch, static = _decode_operands(
        q_s, k_new, v_new, cache_k, cache_v, page_table, lam_p, sub_g, layer=layer, seq_step=seq_step)
    assert batch * n_pairs * nq == bs * dec_steps
    kv_spec = pl.BlockSpec((None, nq, LANES, tq), lambda b, p, i, pt: (b, 0, p, 0))
    q_spec = pl.BlockSpec((tq, LANES), lambda b, p, i, pt: (b * nq + i, p))
    grid_spec = pltpu.PrefetchScalarGridSpec(
        num_scalar_prefetch=1, grid=(batch, n_pairs, nq),
        in_specs=[q_spec, kv_spec, kv_spec, pl.BlockSpec((2 * tq, tq), lambda b, p, i, pt: (0, 0))] + dec_specs,
        out_specs=(q_spec, dec_out), scratch_shapes=scratch)
    o_sb, o_dec = pl.pallas_call(
        functools.partial(_sb_dec_kernel, tq=tq, n_pairs=n_pairs, nq=nq, n_dec=len(args), lam_init=lam_init,
                          **static),
        out_shape=(jax.ShapeDtypeStruct((m, n_pairs * LANES), BF16),
                   jax.ShapeDtypeStruct((bs, 1, width), BF16)),
        grid_spec=grid_spec,
        compiler_params=_cparams(("arbitrary", "arbitrary", "arbitrary")),
        name="attn_sb_decode",
    )(page_table, q_p, ktb, vtb, _suffix_matrix(tq), *args)
    return o_sb, o_dec.reshape(bs, width)


def _ffn_kernel(x_ref, g_ref, mod_ref, win_ref, cw_ref, cb_ref, wout_ref, fg_ref,
                y_ref, ff_ref, gs_ref, *, tiles_per_seq, tm, chunk, final):
    i = pl.program_id(0)
    seq = i // tiles_per_seq
    hidden = wout_ref.shape[0]
    halo = SUBLANES

    @pl.when(i % tiles_per_seq == 0)
    def _():
        gs_ref[0:halo, :] = jnp.zeros((halo, hidden), F32)

    x = x_ref[...]
    sh = _mod_row(mod_ref, 3, seq, False)
    sc = _mod_row(mod_ref, 4, seq, False)
    gate = _mod_row(mod_ref, 5, seq, False)
    h = _norm_mod(x, g_ref[...], sc, sh).astype(BF16)
    acc = jnp.zeros(x.shape, F32)
    n_chunks = hidden // chunk

    def up_proj(c):
        lo, up = c * chunk, (c + 1) * chunk
        return (jnp.dot(h, win_ref[:, lo:up], preferred_element_type=F32),
                jnp.dot(h, win_ref[:, hidden + lo:hidden + up], preferred_element_type=F32))

    nxt = up_proj(0)
    for c in range(n_chunks):
        lo, up = c * chunk, (c + 1) * chunk
        g, u = nxt
        if c + 1 < n_chunks:
            nxt = up_proj(c + 1)
        gs_ref[halo:halo + tm, lo:up] = g
        g1 = gs_ref[halo - 1:halo - 1 + tm, lo:up]
        g2 = gs_ref[halo - 2:halo - 2 + tm, lo:up]
        gc = cw_ref[0:1, lo:up] * g2 + cw_ref[1:2, lo:up] * g1 + cw_ref[2:3, lo:up] * g + cb_ref[:, lo:up]
        act = (_silu(gc) * u).astype(BF16)
        acc = acc + jnp.dot(act, wout_ref[lo:up, :], preferred_element_type=F32)
    ff_ref[...] = gs_ref[halo + tm - (FFN_CONV - 1):halo + tm, :]
    gs_ref[0:halo, :] = gs_ref[tm:tm + halo, :]
    y = x + gate * acc
    if final:
        ms = jnp.mean(y * y, axis=-1, keepdims=True)
        y = y * lax.rsqrt(ms + EPS) * fg_ref[...]
    y_ref[...] = y


def _ffn_prompt(x, norm_g, mod, win_bf, conv_w, conv_b, wout_bf, final_g, *, batch, seq_len, final):
    m, d = x.shape
    hidden = wout_bf.shape[0]
    tm = TM_FFN
    tps = seq_len // tm
    return pl.pallas_call(
        functools.partial(_ffn_kernel, tiles_per_seq=tps, tm=tm, chunk=FFN_CHUNK, final=final),
        out_shape=(jax.ShapeDtypeStruct((m, d), F32),
                   jax.ShapeDtypeStruct((batch, FFN_CONV - 1, hidden), F32)),
        grid=(m // tm,),
        in_specs=[
            pl.BlockSpec((tm, d), lambda i: (i, 0)),
            _resident((1, d)),
            _resident(mod.shape),
            _resident(win_bf.shape),
            _resident(conv_w.shape),
            _resident((1, hidden)),
            _resident(wout_bf.shape),
            _resident((1, d)),
        ],
        out_specs=(pl.BlockSpec((tm, d), lambda i: (i, 0)),
                   pl.BlockSpec((None, FFN_CONV - 1, hidden), lambda i: (i // tps, 0, 0))),
        scratch_shapes=[pltpu.VMEM((tm + SUBLANES, hidden), F32)],
        compiler_params=_cparams(("arbitrary",)),
        name="conv_ffn",
    )(x, norm_g.reshape(1, d), mod, win_bf, conv_w, conv_b.reshape(1, hidden), wout_bf,
      final_g.reshape(1, d))


def _ffn_dec_kernel(x_ref, g_ref, mod_ref, wg_ref, wu_ref, past_ref, cw_ref, cb_ref, wout_ref, fg_ref,
                    y_ref, gnew_ref, h_scr, acc_scr, *, final):
    c = pl.program_id(0)

    @pl.when(c == 0)
    def _():
        h = _norm_mod(x_ref[...], g_ref[...], mod_ref[4], mod_ref[3])
        h_scr[...] = h.astype(BF16)
        acc_scr[...] = jnp.zeros_like(acc_scr)

    h = h_scr[...]
    g = jnp.dot(h, wg_ref[...], preferred_element_type=F32)
    u = jnp.dot(h, wu_ref[...], preferred_element_type=F32)
    gnew_ref[...] = g
    gc = cw_ref[0:1, :] * past_ref[0] + cw_ref[1:2, :] * past_ref[1] + cw_ref[2:3, :] * g + cb_ref[...]
    act = (_silu(gc) * u).astype(BF16)
    acc_scr[...] = acc_scr[...] + jnp.dot(act, wout_ref[...], preferred_element_type=F32)

    @pl.when(c == pl.num_programs(0) - 1)
    def _():
        y = x_ref[...] + mod_ref[5] * acc_scr[...]
        if final:
            ms = jnp.mean(y * y, axis=-1, keepdims=True)
            y = y * lax.rsqrt(ms + EPS) * fg_ref[...]
        y_ref[...] = y


def _ffn_decode(x, norm_g, mod, win_bf, past_t, conv_w, conv_b, wout_bf, final_g, *, final):
    m, d = x.shape
    hidden = wout_bf.shape[0]
    chunk = FFN_CHUNK_DEC
    n = hidden // chunk
    return pl.pallas_call(
        functools.partial(_ffn_dec_kernel, final=final),
        out_shape=(jax.ShapeDtypeStruct((m, d), F32), jax.ShapeDtypeStruct((m, hidden), F32)),
        grid=(n,),
        in_specs=[
            pl.BlockSpec((m, d), lambda c: (0, 0)),
            pl.BlockSpec((1, d), lambda c: (0, 0)),
            pl.BlockSpec(mod.shape, lambda c: (0, 0, 0)),
            pl.BlockSpec((d, chunk), lambda c: (0, c)),
            pl.BlockSpec((d, chunk), lambda c: (0, n + c)),
            pl.BlockSpec((FFN_CONV - 1, m, chunk), lambda c: (0, 0, c)),
            pl.BlockSpec((FFN_CONV, chunk), lambda c: (0, c)),
            pl.BlockSpec((1, chunk), lambda c: (0, c)),
            pl.BlockSpec((chunk, d), lambda c: (c, 0)),
            pl.BlockSpec((1, d), lambda c: (0, 0)),
        ],
        out_specs=(pl.BlockSpec((m, d), lambda c: (0, 0)),
                   pl.BlockSpec((m, chunk), lambda c: (0, c))),
        scratch_shapes=[pltpu.VMEM((m, d), BF16), pltpu.VMEM((m, d), F32)],
        compiler_params=_cparams(("arbitrary",)),
        name="conv_ffn_decode",
    )(x, norm_g.reshape(1, d), mod, win_bf, win_bf, past_t, conv_w, conv_b.reshape(1, hidden),
      wout_bf, final_g.reshape(1, d))


def _mixer_kernel(x_ref, g_ref, mod_ref, win_ref, clg_ref, clb_ref, ws_ref, bs_ref, dw_ref, db_ref,
                  dlg_ref, dlb_ref, wout_ref, y_ref, cv_ref, cd_ref, a_scr, sh_scr,
                  *, tiles_per_seq, tm, width):
    i = pl.program_id(0)
    seq = i // tiles_per_seq
    halo = 4 * SUBLANES
    taps = D_CONV

    @pl.when(i % tiles_per_seq == 0)
    def _():
        a_scr[0:halo, :] = jnp.zeros((halo, width), F32)

    x = x_ref[...]
    sh = _mod_row(mod_ref, 0, seq, False)
    sc = _mod_row(mod_ref, 1, seq, False)
    gate = _mod_row(mod_ref, 2, seq, False)
    h = _norm_mod(x, g_ref[...], sc, sh).astype(BF16)

    zu, zv, za, zb = (jnp.dot(h, win_ref[:, n * width:(n + 1) * width], preferred_element_type=F32)
                      for n in range(4))

    u = _gelu_tanh(zu)
    v = _layer_norm(_gelu_tanh(zv), clg_ref[...], clb_ref[...])
    cv_ref[...] = v[tm - CHUNK:tm, :]
    vb = v.astype(BF16)
    gw = width // C_GROUPS
    r = lax.broadcasted_iota(jnp.int32, (CHUNK, CHUNK), 0)
    s = lax.broadcasted_iota(jnp.int32, (CHUNK, CHUNK), 1)
    mixed_rows = []
    for ch in range(tm // CHUNK):
        cols = []
        for grp in range(C_GROUPS):
            wm = jnp.where(s <= r, ws_ref[grp], 0.0).astype(BF16)
            vg = vb[ch * CHUNK:(ch + 1) * CHUNK, grp * gw:(grp + 1) * gw]
            cols.append(jnp.dot(wm, vg, preferred_element_type=F32))
        mixed_rows.append(jnp.concatenate(cols, axis=1) + bs_ref[...])
    s_out = u * jnp.concatenate(mixed_rows, axis=0)

    a = za * jax.nn.sigmoid(zb)
    a_scr[halo:halo + tm, :] = a
    base = halo - (taps - 1)
    dc = jnp.zeros((tm, width), F32) + db_ref[...]
    for s in range(SUBLANES):
        ks = list(range(s, taps, SUBLANES))
        rows = tm + SUBLANES * (len(ks) - 1)
        sh_scr[0:rows, :] = a_scr[base + s:base + s + rows, :]
        for n, k in enumerate(ks):
            dc = dc + dw_ref[k:k + 1, :] * sh_scr[SUBLANES * n:SUBLANES * n + tm, :]
    cd_ref[...] = a_scr[halo + tm - (taps - 1):halo + tm, :]
    a_scr[0:halo, :] = a_scr[tm:tm + halo, :]
    d_out = _silu(_layer_norm(dc, dlg_ref[...], dlb_ref[...]))

    o = jnp.dot(s_out.astype(BF16), wout_ref[0:width, :], preferred_element_type=F32)
    o = o + jnp.dot(d_out.astype(BF16), wout_ref[width:2 * width, :], preferred_element_type=F32)
    y_ref[...] = x + gate * o


def _mixer_prompt(x, norm_g, mod, win_bf, c_ln_g, c_ln_b, ws, bs, dw, db, d_ln_g, d_ln_b, wout_bf,
                  *, batch, seq_len):
    m, d = x.shape
    width = win_bf.shape[1] // 4
    tm = TM_MIX
    tps = seq_len // tm
    gw = width // C_GROUPS
    bs_exp = jnp.repeat(bs.T, gw, axis=1)
    row = lambda a: a.reshape(1, width)
    return pl.pallas_call(
        functools.partial(_mixer_kernel, tiles_per_seq=tps, tm=tm, width=width),
        out_shape=(jax.ShapeDtypeStruct((m, d), F32),
                   jax.ShapeDtypeStruct((batch, CHUNK, width), F32),
                   jax.ShapeDtypeStruct((batch, D_CONV - 1, width), F32)),
        grid=(m // tm,),
        in_specs=[
            pl.BlockSpec((tm, d), lambda i: (i, 0)),
            _resident((1, d)),
            _resident(mod.shape),
            _resident(win_bf.shape),
            _resident((1, width)), _resident((1, width)),
            _resident(ws.shape),
            _resident(bs_exp.shape),
            _resident(dw.shape),
            _resident((1, width)), _resident((1, width)), _resident((1, width)),
            _resident(wout_bf.shape),
        ],
        out_specs=(pl.BlockSpec((tm, d), lambda i: (i, 0)),
                   pl.BlockSpec((None, CHUNK, width), lambda i: (i // tps, 0, 0)),
                   pl.BlockSpec((None, D_CONV - 1, width), lambda i: (i // tps, 0, 0))),
        scratch_shapes=[pltpu.VMEM((tm + 4 * SUBLANES, width), F32),
                        pltpu.VMEM((tm + 3 * SUBLANES, width), F32)],
        compiler_params=_cparams(("arbitrary",)),
        name="mixer",
    )(x, norm_g.reshape(1, d), mod, win_bf, row(c_ln_g), row(c_ln_b), ws, bs_exp, dw, row(db),
      row(d_ln_g), row(d_ln_b), wout_bf)


def _mixer_dec_kernel(x_ref, g_ref, mod_ref, win_ref, clg_ref, clb_ref, w0_ref, b0_ref, past_ref,
                      dw_ref, db_ref, dlg_ref, dlb_ref, wout_ref, y_ref, v_ref, a_ref, *, width):
    x = x_ref[...]
    h = _norm_mod(x, g_ref[...], mod_ref[1], mod_ref[0]).astype(BF16)
    u = _gelu_tanh(jnp.dot(h, win_ref[:, 0:width], preferred_element_type=F32))
    v = _gelu_tanh(jnp.dot(h, win_ref[:, width:2 * width], preferred_element_type=F32))
    v = _layer_norm(v, clg_ref[...], clb_ref[...])
    v_ref[...] = v
    mixed = w0_ref[...] * v + b0_ref[...]
    s_out = u * mixed
    za = jnp.dot(h, win_ref[:, 2 * width:3 * width], preferred_element_type=F32)
    zb = jnp.dot(h, win_ref[:, 3 * width:4 * width], preferred_element_type=F32)
    a = za * jax.nn.sigmoid(zb)
    a_ref[...] = a
    dc = dw_ref[D_CONV - 1:D_CONV, :] * a + db_ref[...]
    for k in range(D_CONV - 1):
        dc = dc + dw_ref[k:k + 1, :] * past_ref[k]
    d_out = _silu(_layer_norm(dc, dlg_ref[...], dlb_ref[...]))
    o = jnp.dot(s_out.astype(BF16), wout_ref[0:width, :], preferred_element_type=F32)
    o = o + jnp.dot(d_out.astype(BF16), wout_ref[width:2 * width, :], preferred_element_type=F32)
    y_ref[...] = x + mod_ref[2] * o


def _mixer_decode(x, norm_g, mod, win_bf, c_ln_g, c_ln_b, ws, bs, past_t, dw, db, d_ln_g, d_ln_b, wout_bf):
    m, d = x.shape
    width = win_bf.shape[1] // 4
    gw = width // C_GROUPS
    row = lambda a: a.reshape(1, width)
    w0 = jnp.repeat(ws[:, 0, 0], gw).reshape(1, width)
    b0 = jnp.repeat(bs[:, 0], gw).reshape(1, width)
    args = (x, norm_g.reshape(1, d), mod, win_bf, row(c_ln_g), row(c_ln_b), w0, b0, past_t, dw, row(db),
            row(d_ln_g), row(d_ln_b), wout_bf)
    full = lambda a: pl.BlockSpec(a.shape, lambda i, nd=a.ndim: (0,) * nd)
    return pl.pallas_call(
        functools.partial(_mixer_dec_kernel, width=width),
        out_shape=(jax.ShapeDtypeStruct((m, d), F32),
                   jax.ShapeDtypeStruct((m, width), F32),
                   jax.ShapeDtypeStruct((m, width), F32)),
        grid=(1,),
        in_specs=[full(a) for a in args],
        out_specs=(pl.BlockSpec((m, d), lambda i: (0, 0)),
                   pl.BlockSpec((m, width), lambda i: (0, 0)),
                   pl.BlockSpec((m, width), lambda i: (0, 0))),
        compiler_params=_cparams(("arbitrary",)),
        name="mixer_decode",
    )(*args)


def kernel(x_prompt, x_sample, c_prompt, c_sample, cache_k, cache_v, state_conv_d, state_ffn_conv,
           page_table, norm_g, ada_w, ada_b, att_w_in, att_lam, att_sub_g, att_w_out, mix_w_in,
           c_ln_g, c_ln_b, c_ws, c_bs, d_conv_w, d_conv_b, d_ln_g, d_ln_b, mix_w_out,
           ffn_w_in, ffn_conv_w, ffn_conv_b, ffn_w_out, final_g):
    batch, seq_len, d = x_prompt.shape
    bs = x_sample.shape[0]
    depth = ada_w.shape[0]
    att_width = att_w_out.shape[1]
    n_heads = att_width // HEAD_DIM
    hidden = ffn_w_out.shape[1]
    assert x_sample.shape[1] == 1 and seq_len % TQ == 0 and seq_len % TM_PROJ == 0
    assert bs % (2 * SUBLANES) == 0 and page_table.shape[1] % PAGES_PER_STEP == 0

    rows = bs + batch
    pad = (-rows) % (2 * SUBLANES)
    c_all = jnp.concatenate([c_sample, c_prompt, jnp.zeros((pad, d), F32)], axis=0)
    mod = _ada_mod(c_all, ada_w, ada_b)
    mod_s = mod[:, :, 0:bs]
    mod_p = mod[:, :, bs:bs + batch]

    xp = x_prompt.reshape(batch * seq_len, d)
    xs = x_sample.reshape(bs, d)
    k_s, v_s, cv_p, cv_s, cd_p, cd_s, ff_p, ff_s = ([] for _ in range(8))
    kt_p = vt_p = None

    for i in range(depth):
        j = i // 2
        if i % 2 == 0:
            lam_init = 0.8 - 0.6 * math.exp(-0.3 * i)
            w_in = att_w_in[j].astype(BF16)
            w_out = att_w_out[j].astype(BF16)
            n_sb = N_SB_HEADS * HEAD_DIM
            wq = w_in[:, 0:att_width]
            wkt = w_in[:, att_width:2 * att_width].T
            wvt = w_in[:, 2 * att_width:3 * att_width].T
            q, kt_p, vt_p, ktb, vtb = _qkv_prompt(xp, norm_g[i, 0], mod_p[i], wq, wkt, wvt, kt_p, vt_p,
                                                  batch=batch, seq_len=seq_len, tm=TQ)
            q_s, k, v = _qkv_decode(xs, norm_g[i, 0], mod_s[i], w_in)
            if _fused_steps_match(batch, seq_len, TQ, bs, page_table.shape[1]):
                o_sb, o = _attn_sb_decode(q, ktb, vtb, q_s, k, v, cache_k, cache_v, page_table, att_lam[j],
                                          att_sub_g[j], batch=batch, seq_len=seq_len, tq=TQ, layer=j,
                                          lam_init=lam_init)
            else:
                o_sb = _attn_sb(q, ktb, vtb, batch=batch, seq_len=seq_len, tq=TQ)
                o = _decode_attn(q_s, k, v, cache_k, cache_v, page_table, att_lam[j], att_sub_g[j],
                                 layer=j, lam_init=lam_init)
            o_d = _attn_diff(q, ktb, vtb, att_lam[j], att_sub_g[j], batch=batch, seq_len=seq_len, tq=TQ,
                             lam_init=lam_init)
            xp = _out_proj(xp, o_sb, o_d, mod_p[i], w_out, tm=TM_PROJ,
                           tiles_per_seq=seq_len // TM_PROJ, per_row=False)
            xs = _out_proj(xs, o[:, 0:n_sb], o[:, n_sb:], mod_s[i], w_out, tm=bs, tiles_per_seq=1,
                           per_row=True)
            k_s.append(k.reshape(bs, 1, n_heads, HEAD_DIM))
            v_s.append(v.reshape(bs, 1, n_heads, HEAD_DIM))
        else:
            w_in = mix_w_in[j].astype(BF16)
            w_out = mix_w_out[j].astype(BF16)
            xp, cv, cd = _mixer_prompt(xp, norm_g[i, 0], mod_p[i], w_in, c_ln_g[j], c_ln_b[j], c_ws[j],
                                       c_bs[j], d_conv_w[j], d_conv_b[j], d_ln_g[j], d_ln_b[j], w_out,
                                       batch=batch, seq_len=seq_len)
            cv_p.append(cv)
            cd_p.append(cd)
            past = state_conv_d[j]
            xs, v_new, a_new = _mixer_decode(xs, norm_g[i, 0], mod_s[i], w_in, c_ln_g[j], c_ln_b[j],
                                             c_ws[j], c_bs[j], past.transpose(1, 0, 2), d_conv_w[j],
                                             d_conv_b[j], d_ln_g[j], d_ln_b[j], w_out)
            cv_s.append(v_new[:, None, :])
            cd_s.append(jnp.concatenate([past[:, 1:], a_new[:, None, :]], axis=1))
        final = i == depth - 1
        w_in = ffn_w_in[i].astype(BF16)
        w_out = ffn_w_out[i].astype(BF16)
        xp, ff = _ffn_prompt(xp, norm_g[i, 1], mod_p[i], w_in, ffn_conv_w[i], ffn_conv_b[i], w_out,
                             final_g, batch=batch, seq_len=seq_len, final=final)
        ff_p.append(ff)
        past = state_ffn_conv[i]
        xs, g_new = _ffn_decode(xs, norm_g[i, 1], mod_s[i], w_in, past.transpose(1, 0, 2), ffn_conv_w[i],
                                ffn_conv_b[i], w_out, final_g, final=final)
        ff_s.append(jnp.concatenate([past[:, 1:], g_new[:, None, :]], axis=1))

    def to_cache_layout(t):
        return t.reshape(t.shape[0], batch, n_heads, HEAD_DIM, seq_len).transpose(0, 1, 4, 2, 3)

    return (xp.reshape(batch, seq_len, d), xs.reshape(bs, 1, d),
            to_cache_layout(kt_p), to_cache_layout(vt_p), jnp.stack(k_s), jnp.stack(v_s),
            jnp.stack(cv_p), jnp.stack(cv_s), jnp.stack(cd_p), jnp.stack(cd_s),
            jnp.stack(ff_p), jnp.stack(ff_s))
```

```python
import functools
import math

import jax
import jax.numpy as jnp
from jax import lax
from jax.experimental import pallas as pl
from jax.experimental.pallas import tpu as pltpu

F32 = jnp.float32
BF16 = jnp.bfloat16

HEAD_DIM = 64
N_SB_HEADS = 8
N_DIFF_HEADS = 4
N_MOD = 6
C_GROUPS = 4
CHUNK = 128
D_CONV = 31
FFN_CONV = 3
EPS = 1e-6
NEG_INF = -1e30
SB_DEAD = 110.0

LANES = 128
SUBLANES = 8
VMEM_LIMIT = 56 * 1024 * 1024

TM_PROJ = 512
TM_FFN = 256
TM_MIX = 256
TQ = 256
PAGES_PER_STEP = 16
FFN_CHUNK = 256
FFN_CHUNK_DEC = 1408


def _cparams(sem, vmem=VMEM_LIMIT):
    return pltpu.CompilerParams(dimension_semantics=sem, vmem_limit_bytes=vmem)


def _resident(shape):
    nd = len(shape)
    return pl.BlockSpec(shape, lambda *_: (0,) * nd, pipeline_mode=pl.Buffered(1))


def _norm_mod(x, g, sc, sh):
    ms = jnp.mean(x * x, axis=-1, keepdims=True)
    y = x * lax.rsqrt(ms + EPS) * g
    return y * (1.0 + sc) + sh


def _mod_row(mod_ref, slot, seq, per_row):
    if per_row:
        return mod_ref[slot]
    return mod_ref[slot, pl.ds(seq, 1), :]


def _layer_norm(x, g, b):
    mu = jnp.mean(x, axis=-1, keepdims=True)
    xc = x - mu
    var = jnp.mean(xc * xc, axis=-1, keepdims=True)
    return xc * lax.rsqrt(var + EPS) * g + b


def _gelu_tanh(x):
    c = math.sqrt(2.0 / math.pi)
    return x * (0.5 * (1.0 + jnp.tanh(c * (x + 0.044715 * (x * x * x)))))


def _silu(x):
    return x * jax.nn.sigmoid(x)


def _softplus(z):
    return jnp.maximum(z, 0.0) + jnp.log(1.0 + jnp.exp(-jnp.abs(z)))


def _split_bf16(x):
    hi = x.astype(BF16)
    lo = (x - hi.astype(F32)).astype(BF16)
    return hi, lo


def _ada_kernel(c_ref, w_ref, b_ref, o_ref):
    c = c_ref[...]
    s = _silu(c).astype(BF16)
    w = w_ref[...].astype(BF16)
    o_ref[...] = jnp.dot(s, w, preferred_element_type=F32) + b_ref[...]


def _ada_mod(c_all, ada_w, ada_b):
    depth, d, _ = ada_w.shape
    rows = c_all.shape[0]
    return pl.pallas_call(
        _ada_kernel,
        out_shape=jax.ShapeDtypeStruct((depth, N_MOD, rows, d), F32),
        grid=(depth, N_MOD),
        in_specs=[
            pl.BlockSpec((rows, d), lambda i, j: (0, 0)),
            pl.BlockSpec((None, d, d), lambda i, j: (i, 0, j)),
            pl.BlockSpec((None, None, 1, d), lambda i, j: (i, j, 0, 0)),
        ],
        out_specs=pl.BlockSpec((None, None, rows, d), lambda i, j: (i, j, 0, 0)),
        compiler_params=_cparams(("parallel", "parallel")),
        name="ada_mod",
    )(c_all, ada_w, ada_b.reshape(depth, N_MOD, 1, d))


def _qkv_dec_kernel(x_ref, g_ref, mod_ref, w_ref, q_ref, k_ref, v_ref, *, width):
    h = _norm_mod(x_ref[...], g_ref[...], mod_ref[1], mod_ref[0]).astype(BF16)
    q = jnp.dot(h, w_ref[:, 0:width], preferred_element_type=F32)
    q_ref[...] = (q * (HEAD_DIM ** -0.5)).astype(BF16)
    k_ref[...] = jnp.dot(h, w_ref[:, width:2 * width], preferred_element_type=F32)
    v_ref[...] = jnp.dot(h, w_ref[:, 2 * width:3 * width], preferred_element_type=F32)


def _qkv_decode(x, norm_g, mod, w_bf):
    m, d = x.shape
    width = w_bf.shape[1] // 3
    tile = lambda dt: jax.ShapeDtypeStruct((m, width), dt)
    full = lambda shape: pl.BlockSpec(shape, lambda i: (0,) * len(shape))
    return pl.pallas_call(
        functools.partial(_qkv_dec_kernel, width=width),
        out_shape=(tile(BF16), tile(F32), tile(F32)),
        grid=(1,),
        in_specs=[full((m, d)), full((1, d)), full(mod.shape), full(w_bf.shape)],
        out_specs=(full((m, width)),) * 3,
        compiler_params=_cparams(("arbitrary",)),
        name="qkv_decode",
    )(x, norm_g.reshape(1, d), mod, w_bf)


def _qkv_t_kernel(x_ref, g_ref, mod_ref, wq_ref, wkt_ref, wvt_ref, *rest, tiles_per_seq, n_prev):
    if n_prev:
        pk_ref, pv_ref = rest[0:2]
        rest = rest[2:]
    q_ref, kt_ref, vt_ref, ktb_ref, vtb_ref = rest
    seq = pl.program_id(0) // tiles_per_seq
    sh = _mod_row(mod_ref, 0, seq, False)
    sc = _mod_row(mod_ref, 1, seq, False)
    h = _norm_mod(x_ref[...], g_ref[...], sc, sh).astype(BF16)
    q = jnp.dot(h, wq_ref[...], preferred_element_type=F32)
    q_ref[...] = (q * (HEAD_DIM ** -0.5)).astype(BF16)
    nt = (((1,), (1,)), ((), ()))
    kt = lax.dot_general(wkt_ref[...], h, nt, preferred_element_type=F32)
    vt = lax.dot_general(wvt_ref[...], h, nt, preferred_element_type=F32)
    if n_prev:
        kt_ref[0:n_prev] = pk_ref[...]
        vt_ref[0:n_prev] = pv_ref[...]
    kt_ref[n_prev] = kt
    vt_ref[n_prev] = vt
    ktb_ref[...] = kt.astype(BF16)
    vtb_ref[...] = vt.astype(BF16)


def _qkv_prompt(x, norm_g, mod, wq, wkt, wvt, prev_k, prev_v, *, batch, seq_len, tm):
    m, d = x.shape
    width = wq.shape[1]
    tps = seq_len // tm
    n_prev = 0 if prev_k is None else prev_k.shape[0]
    slab = lambda n: pl.BlockSpec((n, None, width, tm), lambda i: (0, i // tps, 0, i % tps))
    blocked = pl.BlockSpec((None, None, width, tm), lambda i: (i // tps, i % tps, 0, 0))
    in_specs = [
        pl.BlockSpec((tm, d), lambda i: (i, 0)),
        _resident((1, d)),
        _resident(mod.shape),
        _resident(wq.shape), _resident(wkt.shape), _resident(wvt.shape),
    ]
    args = [x, norm_g.reshape(1, d), mod, wq, wkt, wvt]
    if n_prev:
        in_specs += [slab(n_prev), slab(n_prev)]
        args += [prev_k, prev_v]
    stacked = jax.ShapeDtypeStruct((n_prev + 1, batch, width, seq_len), F32)
    blk = jax.ShapeDtypeStruct((batch, tps, width, tm), BF16)
    return pl.pallas_call(
        functools.partial(_qkv_t_kernel, tiles_per_seq=tps, n_prev=n_prev),
        out_shape=(jax.ShapeDtypeStruct((m, width), BF16), stacked, stacked, blk, blk),
        grid=(m // tm,),
        in_specs=in_specs,
        out_specs=(pl.BlockSpec((tm, width), lambda i: (i, 0)), slab(n_prev + 1), slab(n_prev + 1),
                   blocked, blocked),
        compiler_params=_cparams(("parallel",)),
        name="qkv_prompt",
    )(*args)


def _oproj_kernel(x_ref, oa_ref, ob_ref, mod_ref, w_ref, y_ref, *, tiles_per_seq, per_row, split):
    seq = pl.program_id(0) // tiles_per_seq
    gate = _mod_row(mod_ref, 2, seq, per_row)
    o = jnp.dot(oa_ref[...], w_ref[0:split, :], preferred_element_type=F32)
    o = o + jnp.dot(ob_ref[...], w_ref[split:, :], preferred_element_type=F32)
    y_ref[...] = x_ref[...] + gate * o


def _out_proj(x, oa, ob, mod, w_bf, *, tm, tiles_per_seq, per_row):
    m, d = x.shape
    split = oa.shape[1]
    return pl.pallas_call(
        functools.partial(_oproj_kernel, tiles_per_seq=tiles_per_seq, per_row=per_row, split=split),
        out_shape=jax.ShapeDtypeStruct((m, d), F32),
        grid=(m // tm,),
        in_specs=[
            pl.BlockSpec((tm, d), lambda i: (i, 0)),
            pl.BlockSpec((tm, split), lambda i: (i, 0)),
            pl.BlockSpec((tm, ob.shape[1]), lambda i: (i, 0)),
            _resident(mod.shape),
            _resident(w_bf.shape),
        ],
        out_specs=pl.BlockSpec((tm, d), lambda i: (i, 0)),
        compiler_params=_cparams(("parallel",)),
        name="out_proj",
    )(x, oa, ob, mod, w_bf)


_NT = (((1,), (1,)), ((), ()))


def _sb_kernel(q_ref, kt_ref, vt_ref, uu_ref, o_ref, *, tq):
    _sb_body(pl.program_id(2), q_ref, kt_ref, vt_ref, uu_ref, o_ref, tq)


def _sb_body(i, q_ref, kt_ref, vt_ref, uu_ref, o_ref, tq):
    q = q_ref[...]
    lane = lax.broadcasted_iota(jnp.int32, (1, LANES), 1)
    row = lax.broadcasted_iota(jnp.int32, (tq, tq), 0)
    col = lax.broadcasted_iota(jnp.int32, (tq, tq), 1)
    strict = col < row
    uu = uu_ref[...]
    zero_q = jnp.zeros_like(q)
    qs = (jnp.where(lane < HEAD_DIM, q, zero_q), jnp.where(lane >= HEAD_DIM, q, zero_q))

    def blocks(js, state, diag):
        kbs = [kt_ref[j] for j in js]
        vbs = [vt_ref[j] for j in js]
        chains = [(n, h) for n in range(len(js)) for h in range(2)]
        z = {c: jnp.dot(qs[c[1]], kbs[c[0]], preferred_element_type=F32) for c in chains}
        sp = {c: _softplus(z[c]) for c in chains}
        lk = {c: jnp.where(strict, sp[c], 0.0) if (diag and c[0] == 0) else sp[c] for c in chains}
        cs = {c: jnp.dot(jnp.concatenate(_split_bf16(lk[c]), axis=1), uu, preferred_element_type=F32)
              for c in chains}
        tot = {c: jnp.sum(lk[c], axis=1, keepdims=True) for c in chains}
        new = []
        for h in range(2):
            carry, acc = state[2 * h], state[2 * h + 1]
            for n in range(len(js)):
                c = (n, h)
                a = jnp.exp(z[c] - sp[c] - cs[c] - carry)
                if diag and n == 0:
                    a = jnp.where(strict, a, 0.0)
                acc = acc + lax.dot_general(a.astype(BF16), vbs[n], _NT, preferred_element_type=F32)
                carry = carry + tot[c]
            new += [carry, acc]
        return tuple(new)

    c0 = jnp.zeros((tq, 1), F32)
    a0 = jnp.zeros((tq, LANES), F32)
    state = (c0, a0, c0, a0)
    has_prev = jnp.minimum(i, 1)
    state = lax.fori_loop(0, 1 - has_prev, lambda t, st: blocks([i], st, True), state)
    state = lax.fori_loop(0, has_prev, lambda t, st: blocks([i, i - 1], st, True), state)
    rest = jnp.maximum(i - 1, 0)

    def live(st):
        return jnp.min(jnp.minimum(st[0], st[2])) < SB_DEAD

    def pair_cond(c):
        return jnp.logical_and(c[0] < rest // 2, live(c[1]))

    def pair_body(c):
        t, st = c
        return t + 1, blocks([i - 2 - 2 * t, i - 3 - 2 * t], st, False)

    _, state = lax.while_loop(pair_cond, pair_body, (jnp.int32(0), state))
    last = jnp.where(live(state), rest % 2, 0)
    state = lax.fori_loop(0, last, lambda t, st: blocks([0], st, False), state)
    o_ref[...] = jnp.where(lane < HEAD_DIM, state[1], state[3]).astype(o_ref.dtype)


def _suffix_matrix(n):
    j = lax.broadcasted_iota(jnp.int32, (n, n), 0)
    s = lax.broadcasted_iota(jnp.int32, (n, n), 1)
    u = (j > s).astype(BF16)
    return jnp.concatenate([u, u], axis=0)


def _attn_sb(q_bf, ktb, vtb, *, batch, seq_len, tq):
    m = q_bf.shape[0]
    nq = seq_len // tq
    n_pairs = N_SB_HEADS * HEAD_DIM // LANES
    kv_spec = pl.BlockSpec((None, nq, LANES, tq), lambda b, p, i: (b, 0, p, 0))
    return pl.pallas_call(
        functools.partial(_sb_kernel, tq=tq),
        out_shape=jax.ShapeDtypeStruct((m, n_pairs * LANES), BF16),
        grid=(batch, n_pairs, nq),
        in_specs=[
            pl.BlockSpec((tq, LANES), lambda b, p, i: (b * nq + i, p)),
            kv_spec, kv_spec,
            _resident((2 * tq, tq)),
        ],
        out_specs=pl.BlockSpec((tq, LANES), lambda b, p, i: (b * nq + i, p)),
        compiler_params=_cparams(("parallel", "parallel", "parallel")),
        name="attn_sb",
    )(q_bf, ktb, vtb, _suffix_matrix(tq))


def _lam_value(lam_ref, lam_init):
    lp = lam_ref[...]
    t1 = jnp.sum(lp[0:1] * lp[1:2], axis=1, keepdims=True)
    t2 = jnp.sum(lp[2:3] * lp[3:4], axis=1, keepdims=True)
    return jnp.exp(t1) - jnp.exp(t2) + lam_init


def _diff_kernel(lam_ref, g_ref, q_ref, kt_ref, vt_ref, o_ref, *, tq, lam_init):
    i = pl.program_id(2)
    q = q_ref[...]
    lane = lax.broadcasted_iota(jnp.int32, (1, LANES), 1)
    row = lax.broadcasted_iota(jnp.int32, (tq, 2 * tq), 0)
    col = lax.broadcasted_iota(jnp.int32, (tq, 2 * tq), 1)
    causal = jnp.logical_or(col <= row, col >= tq)
    zero_q = jnp.zeros_like(q)
    qs = (jnp.where(lane < HEAD_DIM, q, zero_q), jnp.where(lane >= HEAD_DIM, q, zero_q))

    def blocks(js, state, diag):
        kb = jnp.concatenate([kt_ref[j] for j in js], axis=1)
        vb = jnp.concatenate([vt_ref[j] for j in js], axis=1)
        s = [jnp.dot(qs[c], kb, preferred_element_type=F32) for c in range(2)]
        new = []
        for c in range(2):
            m_old, l, acc = state[3 * c:3 * c + 3]
            sc = jnp.where(causal[:, 0:len(js) * tq], s[c], NEG_INF) if diag else s[c]
            m_new = jnp.maximum(m_old, jnp.max(sc, axis=1, keepdims=True))
            alpha = jnp.exp(m_old - m_new)
            p = jnp.exp(sc - m_new)
            l = alpha * l + jnp.sum(p, axis=1, keepdims=True)
            acc = alpha * acc + lax.dot_general(p.astype(BF16), vb, _NT, preferred_element_type=F32)
            new += [m_new, l, acc]
        return tuple(new)

    m0 = jnp.full((tq, 1), NEG_INF, F32)
    l0 = jnp.zeros((tq, 1), F32)
    a0 = jnp.zeros((tq, LANES), F32)
    state = (m0, l0, a0, m0, l0, a0)
    has_prev = jnp.minimum(i, 1)
    state = lax.fori_loop(0, 1 - has_prev, lambda t, st: blocks([i], st, True), state)
    state = lax.fori_loop(0, has_prev, lambda t, st: blocks([i, i - 1], st, True), state)
    rest = jnp.maximum(i - 1, 0)
    state = lax.fori_loop(0, rest // 2, lambda t, st: blocks([i - 2 - 2 * t, i - 3 - 2 * t], st, False), state)
    state = lax.fori_loop(0, rest % 2, lambda t, st: blocks([0], st, False), state)

    lam = _lam_value(lam_ref, lam_init)
    o = state[2] / state[1] - lam * (state[5] / state[4])
    ms = jnp.mean(o * o, axis=1, keepdims=True)
    o = o * lax.rsqrt(ms + EPS) * g_ref[...] * (1.0 - lam_init)
    o_ref[...] = o.astype(o_ref.dtype)


def _attn_diff(q_bf, ktb, vtb, lam_p, sub_g, *, batch, seq_len, tq, lam_init):
    m = q_bf.shape[0]
    nq = seq_len // tq
    off = N_SB_HEADS * HEAD_DIM // LANES
    kv_spec = pl.BlockSpec((None, nq, LANES, tq), lambda b, p, i: (b, 0, off + p, 0))
    return pl.pallas_call(
        functools.partial(_diff_kernel, tq=tq, lam_init=lam_init),
        out_shape=jax.ShapeDtypeStruct((m, N_DIFF_HEADS * LANES), BF16),
        grid=(batch, N_DIFF_HEADS, nq),
        in_specs=[
            _resident(lam_p.shape),
            _resident((1, LANES)),
            pl.BlockSpec((tq, LANES), lambda b, p, i: (b * nq + i, off + p)),
            kv_spec, kv_spec,
        ],
        out_specs=pl.BlockSpec((tq, LANES), lambda b, p, i: (b * nq + i, p)),
        compiler_params=_cparams(("parallel", "parallel", "parallel")),
        name="attn_diff",
    )(lam_p, sub_g.reshape(1, LANES), q_bf, ktb, vtb)


def _dec_attn_kernel(pt_ref, *refs, n_steps, pages, n_heads, lam_init):
    del pt_ref
    _dec_body(pl.program_id(0), pl.program_id(1), refs[0:7 + 2 * pages], refs[7 + 2 * pages],
              refs[8 + 2 * pages:], n_steps=n_steps, pages=pages, n_heads=n_heads, lam_init=lam_init)


def _dec_body(seq, t, in_refs, o_ref, scratch, *, n_steps, pages, n_heads, lam_init):
    qt_ref, kx_ref, vx_ref, lam_ref, g_ref, uo_ref, msuf_ref = in_refs[0:7]
    k_refs = in_refs[7:7 + pages]
    v_refs = in_refs[7 + pages:7 + 2 * pages]
    s_scr, w_scr, x_scr, acc_scr, qb_scr = scratch
    n_pages = n_steps * pages
    n_rows = n_pages * n_heads
    lane = lax.broadcasted_iota(jnp.int32, (1, LANES), 1)
    rid = lax.broadcasted_iota(jnp.int32, (n_heads, 1), 0)
    is_sb = rid < N_SB_HEADS

    def value_row(wp, h):
        if h < N_SB_HEADS:
            return wp[h:h + 1, :]
        first = N_SB_HEADS + 2 * ((h - N_SB_HEADS) // 2)
        return wp[first:first + 1, :] + wp[first + 1:first + 2, :]

    def q_head(h):
        return qb_scr[h * HEAD_DIM:(h + 1) * HEAD_DIM, :]

    @pl.when(t == 0)
    def _():
        acc_scr[...] = jnp.zeros_like(acc_scr)
        src = lax.broadcasted_iota(jnp.int32, (LANES, LANES), 0)
        pick = jnp.where(src == seq, 1.0, 0.0).astype(BF16)
        qb_scr[...] = jnp.dot(qt_ref[...], pick, preferred_element_type=F32)

    @pl.when(t < n_steps)
    def _():
        for h in range(n_heads):
            qh = q_head(h)
            for r in range(pages):
                row = jnp.sum(k_refs[r][h] * qh, axis=0, keepdims=True)
                s_scr[t * pages + r, pl.ds(h, 1), :] = row

    @pl.when(t == n_steps - 1)
    def _():
        for h in range(n_heads):
            x_scr[pl.ds(h, 1), :] = jnp.sum(kx_ref[h] * q_head(h), axis=0, keepdims=True)
        s3 = s_scr[...]
        s = s3.reshape(n_rows, LANES)
        sp = _softplus(s)
        hi, lo = _split_bf16(sp)
        r1 = jnp.dot(jnp.concatenate([hi, lo], axis=1), uo_ref[...], preferred_element_type=F32)
        cs = r1[:, 0:LANES]
        tot = r1[:, LANES:2 * LANES]
        t1 = tot.astype(BF16)
        rem = tot - t1.astype(F32)
        t2, t3 = _split_bf16(rem)
        c3 = jnp.dot(msuf_ref[...], jnp.concatenate([t1, t2, t3], axis=1), preferred_element_type=F32)
        carry = c3[:, 0:LANES] + c3[:, LANES:2 * LANES] + c3[:, 2 * LANES:3 * LANES]
        a3 = jnp.exp(s - sp - cs - carry).reshape(n_pages, n_heads, LANES)
        sx = jnp.where(lane == seq, x_scr[...], NEG_INF)
        m = jnp.max(jnp.maximum(jnp.max(s3, axis=0), sx), axis=1, keepdims=True)
        p3 = jnp.exp(s3 - m[None])
        px = jnp.exp(sx - m)
        l = jnp.sum(jnp.sum(p3, axis=0) + px, axis=1, keepdims=True)
        lam = _lam_value(lam_ref, lam_init)
        coef = jnp.where(lax.bitwise_and(rid, 1) == 0, 1.0, -lam) / l
        w_scr[...] = jnp.where(is_sb[None], a3, p3 * coef[None])
        x_scr[...] = jnp.where(is_sb, 0.0, px * coef)

    @pl.when(t >= n_steps)
    def _():
        blk = t - n_steps
        wps = [w_scr[blk * pages + r] for r in range(pages)]
        for h in range(n_heads):
            acc = acc_scr[h]
            for r in range(pages):
                acc = acc + value_row(wps[r], h) * v_refs[r][h]
            acc_scr[h] = acc

    @pl.when(t == 2 * n_steps - 1)
    def _():
        wx = x_scr[...]
        for h in range(n_heads):
            acc_scr[h] = acc_scr[h] + value_row(wx, h) * vx_ref[h]
        hi, lo = _split_bf16(acc_scr[...].reshape(n_heads * HEAD_DIM, LANES))
        ones = jnp.ones((SUBLANES, LANES), BF16)
        nt = (((1,), (1,)), ((), ()))
        o = lax.dot_general(ones, hi, nt, preferred_element_type=F32)
        o = (o + lax.dot_general(ones, lo, nt, preferred_element_type=F32))[0:1, :]
        n_sb = N_SB_HEADS * HEAD_DIM
        o_ref[:, 0:n_sb] = o[:, 0:n_sb].astype(o_ref.dtype)
        for hd in range(N_DIFF_HEADS):
            od = o[:, n_sb + hd * LANES:n_sb + (hd + 1) * LANES]
            ms = jnp.mean(od * od, axis=1, keepdims=True)
            od = od * lax.rsqrt(ms + EPS) * g_ref[...] * (1.0 - lam_init)
            o_ref[:, n_sb + hd * LANES:n_sb + (hd + 1) * LANES] = od.astype(o_ref.dtype)


def _suffix_and_total_matrix():
    j = lax.broadcasted_iota(jnp.int32, (LANES, LANES), 0)
    s = lax.broadcasted_iota(jnp.int32, (LANES, LANES), 1)
    u1 = jnp.concatenate([(j > s).astype(BF16), jnp.ones((LANES, LANES), BF16)], axis=1)
    return jnp.concatenate([u1, u1], axis=0)


def _later_pages_matrix(n_pages, n_heads):
    n = n_pages * n_heads
    a = lax.broadcasted_iota(jnp.int32, (n, n), 0)
    b = lax.broadcasted_iota(jnp.int32, (n, n), 1)
    same_head = (a % n_heads) == (b % n_heads)
    return jnp.logical_and(same_head, b // n_heads > a // n_heads).astype(BF16)


def _decode_operands(q_bf, k_new, v_new, cache_k, cache_v, page_table, lam_p, sub_g, *, layer, seq_step):
    bs, width = q_bf.shape
    page, n_heads = cache_k.shape[2], cache_k.shape[3]
    assert page == LANES and cache_k.shape[4] == HEAD_DIM and bs <= LANES
    n_pages = page_table.shape[1]
    pages = PAGES_PER_STEP
    n_steps = n_pages // pages
    ck = cache_k.transpose(0, 1, 3, 4, 2)
    cv = cache_v.transpose(0, 1, 3, 4, 2)
    lanes = lambda a: jnp.pad(a.T, ((0, 0), (0, LANES - bs)))
    qt = lanes(q_bf)
    kx = lanes(k_new).reshape(n_heads, HEAD_DIM, LANES)
    vx = lanes(v_new).reshape(n_heads, HEAD_DIM, LANES)

    def k_map(r):
        def index(*ids):
            seq, t = seq_step(*ids[:-1])
            return (layer, ids[-1][seq, jnp.minimum(t, n_steps - 1) * pages + r], 0, 0, 0)
        return index

    def v_map(r):
        def index(*ids):
            seq, t = seq_step(*ids[:-1])
            return (layer, ids[-1][seq, jnp.maximum(t - n_steps, 0) * pages + r], 0, 0, 0)
        return index

    const = lambda shape: pl.BlockSpec(shape, lambda *ids: (0,) * len(shape))
    page_block = (None, None, n_heads, HEAD_DIM, LANES)
    n_rows = n_pages * n_heads
    in_specs = [const(qt.shape), const(kx.shape), const(vx.shape),
                const(lam_p.shape), const((1, LANES)), const((2 * LANES, 2 * LANES)),
                const((n_rows, n_rows))]
    in_specs += [pl.BlockSpec(page_block, k_map(r)) for r in range(pages)]
    in_specs += [pl.BlockSpec(page_block, v_map(r)) for r in range(pages)]
    args = [qt, kx, vx, lam_p, sub_g.reshape(1, LANES), _suffix_and_total_matrix(),
            _later_pages_matrix(n_pages, n_heads)] + [ck] * pages + [cv] * pages
    out_spec = pl.BlockSpec((None, 1, width), lambda *ids: (seq_step(*ids[:-1])[0], 0, 0))
    scratch = [
        pltpu.VMEM((n_pages, n_heads, LANES), F32),
        pltpu.VMEM((n_pages, n_heads, LANES), F32),
        pltpu.VMEM((n_heads, LANES), F32),
        pltpu.VMEM((n_heads, HEAD_DIM, LANES), F32),
        pltpu.VMEM((width, LANES), F32),
    ]
    static = dict(n_steps=n_steps, pages=pages, n_heads=n_heads)
    return args, in_specs, out_spec, scratch, static


def _decode_attn(q_bf, k_new, v_new, cache_k, cache_v, page_table, lam_p, sub_g, *, layer, lam_init):
    bs, width = q_bf.shape
    args, in_specs, out_spec, scratch, static = _decode_operands(
        q_bf, k_new, v_new, cache_k, cache_v, page_table, lam_p, sub_g, layer=layer,
        seq_step=lambda b, t: (b, t))
    grid_spec = pltpu.PrefetchScalarGridSpec(
        num_scalar_prefetch=1, grid=(bs, 2 * static["n_steps"]),
        in_specs=in_specs, out_specs=out_spec, scratch_shapes=scratch)
    out = pl.pallas_call(
        functools.partial(_dec_attn_kernel, lam_init=lam_init, **static),
        out_shape=jax.ShapeDtypeStruct((bs, 1, width), BF16),
        grid_spec=grid_spec,
        compiler_params=_cparams(("parallel", "arbitrary")),
        name="decode_attn",
    )(page_table, *args)
    return out.reshape(bs, width)


def _sb_dec_kernel(pt_ref, q_ref, kt_ref, vt_ref, uu_ref, *rest, tq, n_pairs, nq, n_dec, n_steps, pages,
                   n_heads, lam_init):
    del pt_ref
    dec_in = rest[0:n_dec]
    o_sb_ref, o_dec_ref = rest[n_dec:n_dec + 2]
    scratch = rest[n_dec + 2:]
    i = pl.program_id(2)
    step = (pl.program_id(0) * n_pairs + pl.program_id(1)) * nq + i
    _sb_body(i, q_ref, kt_ref, vt_ref, uu_ref, o_sb_ref, tq)
    _dec_body(step // (2 * n_steps), step % (2 * n_steps), dec_in, o_dec_ref, scratch,
              n_steps=n_steps, pages=pages, n_heads=n_heads, lam_init=lam_init)


def _fused_steps_match(batch, seq_len, tq, bs, n_pages):
    n_pairs = N_SB_HEADS * HEAD_DIM // LANES
    return batch * n_pairs * (seq_len // tq) == bs * 2 * (n_pages // PAGES_PER_STEP)


def _attn_sb_decode(q_p, ktb, vtb, q_s, k_new, v_new, cache_k, cache_v, page_table, lam_p, sub_g,
                    *, batch, seq_len, tq, layer, lam_init):
    m = q_p.shape[0]
    bs, width = q_s.shape
    nq = seq_len // tq
    n_pairs = N_SB_HEADS * HEAD_DIM // LANES

    dec_steps = 2 * (page_table.shape[1] // PAGES_PER_STEP)

    def seq_step(b, p, i):
        step = (b * n_pairs + p) * nq + i
        return step // dec_steps, step % dec_steps

    args, dec_specs, dec_out, scratch, static = _decode_operands(
        q_s, k_new, v_new, cache_k, cache_v, page_table, lam_p, sub_g, layer=layer, seq_step=seq_step)
    assert batch * n_pairs * nq == bs * dec_steps
    kv_spec = pl.BlockSpec((None, nq, LANES, tq), lambda b, p, i, pt: (b, 0, p, 0))
    q_spec = pl.BlockSpec((tq, LANES), lambda b, p, i, pt: (b * nq + i, p))
    grid_spec = pltpu.PrefetchScalarGridSpec(
        num_scalar_prefetch=1, grid=(batch, n_pairs, nq),
        in_specs=[q_spec, kv_spec, kv_spec, pl.BlockSpec((2 * tq, tq), lambda b, p, i, pt: (0, 0))] + dec_specs,
        out_specs=(q_spec, dec_out), scratch_shapes=scratch)
    o_sb, o_dec = pl.pallas_call(
        functools.partial(_sb_dec_kernel, tq=tq, n_pairs=n_pairs, nq=nq, n_dec=len(args), lam_init=lam_init,
                          **static),
        out_shape=(jax.ShapeDtypeStruct((m, n_pairs * LANES), BF16),
                   jax.ShapeDtypeStruct((bs, 1, width), BF16)),
        grid_spec=grid_spec,
        compiler_params=_cparams(("arbitrary", "arbitrary", "arbitrary")),
        name="attn_sb_decode",
    )(page_table, q_p, ktb, vtb, _suffix_matrix(tq), *args)
    return o_sb, o_dec.reshape(bs, width)


def _ffn_kernel(x_ref, g_ref, mod_ref, win_ref, cw_ref, cb_ref, wout_ref, fg_ref,
                y_ref, ff_ref, gs_ref, *, tiles_per_seq, tm, chunk, final):
    i = pl.program_id(0)
    seq = i // tiles_per_seq
    hidden = wout_ref.shape[0]
    halo = SUBLANES

    @pl.when(i % tiles_per_seq == 0)
    def _():
        gs_ref[0:halo, :] = jnp.zeros((halo, hidden), F32)

    x = x_ref[...]
    sh = _mod_row(mod_ref, 3, seq, False)
    sc = _mod_row(mod_ref, 4, seq, False)
    gate = _mod_row(mod_ref, 5, seq, False)
    h = _norm_mod(x, g_ref[...], sc, sh).astype(BF16)
    acc = jnp.zeros(x.shape, F32)
    n_chunks = hidden // chunk

    def up_proj(c):
        lo, up = c * chunk, (c + 1) * chunk
        return (jnp.dot(h, win_ref[:, lo:up], preferred_element_type=F32),
                jnp.dot(h, win_ref[:, hidden + lo:hidden + up], preferred_element_type=F32))

    nxt = up_proj(0)
    for c in range(n_chunks):
        lo, up = c * chunk, (c + 1) * chunk
        g, u = nxt
        if c + 1 < n_chunks:
            nxt = up_proj(c + 1)
        gs_ref[halo:halo + tm, lo:up] = g
        g1 = gs_ref[halo - 1:halo - 1 + tm, lo:up]
        g2 = gs_ref[halo - 2:halo - 2 + tm, lo:up]
        gc = cw_ref[0:1, lo:up] * g2 + cw_ref[1:2, lo:up] * g1 + cw_ref[2:3, lo:up] * g + cb_ref[:, lo:up]
        act = (_silu(gc) * u).astype(BF16)
        acc = acc + jnp.dot(act, wout_ref[lo:up, :], preferred_element_type=F32)
    ff_ref[...] = gs_ref[halo + tm - (FFN_CONV - 1):halo + tm, :]
    gs_ref[0:halo, :] = gs_ref[tm:tm + halo, :]
    y = x + gate * acc
    if final:
        ms = jnp.mean(y * y, axis=-1, keepdims=True)
        y = y * lax.rsqrt(ms + EPS) * fg_ref[...]
    y_ref[...] = y


def _ffn_prompt(x, norm_g, mod, win_bf, conv_w, conv_b, wout_bf, final_g, *, batch, seq_len, final):
    m, d = x.shape
    hidden = wout_bf.shape[0]
    tm = TM_FFN
    tps = seq_len // tm
    return pl.pallas_call(
        functools.partial(_ffn_kernel, tiles_per_seq=tps, tm=tm, chunk=FFN_CHUNK, final=final),
        out_shape=(jax.ShapeDtypeStruct((m, d), F32),
                   jax.ShapeDtypeStruct((batch, FFN_CONV - 1, hidden), F32)),
        grid=(m // tm,),
        in_specs=[
            pl.BlockSpec((tm, d), lambda i: (i, 0)),
            _resident((1, d)),
            _resident(mod.shape),
            _resident(win_bf.shape),
            _resident(conv_w.shape),
            _resident((1, hidden)),
            _resident(wout_bf.shape),
            _resident((1, d)),
        ],
        out_specs=(pl.BlockSpec((tm, d), lambda i: (i, 0)),
                   pl.BlockSpec((None, FFN_CONV - 1, hidden), lambda i: (i // tps, 0, 0))),
        scratch_shapes=[pltpu.VMEM((tm + SUBLANES, hidden), F32)],
        compiler_params=_cparams(("arbitrary",)),
        name="conv_ffn",
    )(x, norm_g.reshape(1, d), mod, win_bf, conv_w, conv_b.reshape(1, hidden), wout_bf,
      final_g.reshape(1, d))


def _ffn_dec_kernel(x_ref, g_ref, mod_ref, wg_ref, wu_ref, past_ref, cw_ref, cb_ref, wout_ref, fg_ref,
                    y_ref, gnew_ref, h_scr, acc_scr, *, final):
    c = pl.program_id(0)

    @pl.when(c == 0)
    def _():
        h = _norm_mod(x_ref[...], g_ref[...], mod_ref[4], mod_ref[3])
        h_scr[...] = h.astype(BF16)
        acc_scr[...] = jnp.zeros_like(acc_scr)

    h = h_scr[...]
    g = jnp.dot(h, wg_ref[...], preferred_element_type=F32)
    u = jnp.dot(h, wu_ref[...], preferred_element_type=F32)
    gnew_ref[...] = g
    gc = cw_ref[0:1, :] * past_ref[0] + cw_ref[1:2, :] * past_ref[1] + cw_ref[2:3, :] * g + cb_ref[...]
    act = (_silu(gc) * u).astype(BF16)
    acc_scr[...] = acc_scr[...] + jnp.dot(act, wout_ref[...], preferred_element_type=F32)

    @pl.when(c == pl.num_programs(0) - 1)
    def _():
        y = x_ref[...] + mod_ref[5] * acc_scr[...]
        if final:
            ms = jnp.mean(y * y, axis=-1, keepdims=True)
            y = y * lax.rsqrt(ms + EPS) * fg_ref[...]
        y_ref[...] = y


def _ffn_decode(x, norm_g, mod, win_bf, past_t, conv_w, conv_b, wout_bf, final_g, *, final):
    m, d = x.shape
    hidden = wout_bf.shape[0]
    chunk = FFN_CHUNK_DEC
    n = hidden // chunk
    return pl.pallas_call(
        functools.partial(_ffn_dec_kernel, final=final),
        out_shape=(jax.ShapeDtypeStruct((m, d), F32), jax.ShapeDtypeStruct((m, hidden), F32)),
        grid=(n,),
        in_specs=[
            pl.BlockSpec((m, d), lambda c: (0, 0)),
            pl.BlockSpec((1, d), lambda c: (0, 0)),
            pl.BlockSpec(mod.shape, lambda c: (0, 0, 0)),
            pl.BlockSpec((d, chunk), lambda c: (0, c)),
            pl.BlockSpec((d, chunk), lambda c: (0, n + c)),
            pl.BlockSpec((FFN_CONV - 1, m, chunk), lambda c: (0, 0, c)),
            pl.BlockSpec((FFN_CONV, chunk), lambda c: (0, c)),
            pl.BlockSpec((1, chunk), lambda c: (0, c)),
            pl.BlockSpec((chunk, d), lambda c: (c, 0)),
            pl.BlockSpec((1, d), lambda c: (0, 0)),
        ],
        out_specs=(pl.BlockSpec((m, d), lambda c: (0, 0)),
                   pl.BlockSpec((m, chunk), lambda c: (0, c))),
        scratch_shapes=[pltpu.VMEM((m, d), BF16), pltpu.VMEM((m, d), F32)],
        compiler_params=_cparams(("arbitrary",)),
        name="conv_ffn_decode",
    )(x, norm_g.reshape(1, d), mod, win_bf, win_bf, past_t, conv_w, conv_b.reshape(1, hidden),
      wout_bf, final_g.reshape(1, d))


def _mixer_kernel(x_ref, g_ref, mod_ref, win_ref, clg_ref, clb_ref, ws_ref, bs_ref, dw_ref, db_ref,
                  dlg_ref, dlb_ref, wout_ref, y_ref, cv_ref, cd_ref, a_scr, sh_scr,
                  *, tiles_per_seq, tm, width):
    i = pl.program_id(0)
    seq = i // tiles_per_seq
    halo = 4 * SUBLANES
    taps = D_CONV

    @pl.when(i % tiles_per_seq == 0)
    def _():
        a_scr[0:halo, :] = jnp.zeros((halo, width), F32)

    x = x_ref[...]
    sh = _mod_row(mod_ref, 0, seq, False)
    sc = _mod_row(mod_ref, 1, seq, False)
    gate = _mod_row(mod_ref, 2, seq, False)
    h = _norm_mod(x, g_ref[...], sc, sh).astype(BF16)

    zu, zv, za, zb = (jnp.dot(h, win_ref[:, n * width:(n + 1) * width], preferred_element_type=F32)
                      for n in range(4))

    u = _gelu_tanh(zu)
    v = _layer_norm(_gelu_tanh(zv), clg_ref[...], clb_ref[...])
    cv_ref[...] = v[tm - CHUNK:tm, :]
    vb = v.astype(BF16)
    gw = width // C_GROUPS
    r = lax.broadcasted_iota(jnp.int32, (CHUNK, CHUNK), 0)
    s = lax.broadcasted_iota(jnp.int32, (CHUNK, CHUNK), 1)
    mixed_rows = []
    for ch in range(tm // CHUNK):
        cols = []
        for grp in range(C_GROUPS):
            wm = jnp.where(s <= r, ws_ref[grp], 0.0).astype(BF16)
            vg = vb[ch * CHUNK:(ch + 1) * CHUNK, grp * gw:(grp + 1) * gw]
            cols.append(jnp.dot(wm, vg, preferred_element_type=F32))
        mixed_rows.append(jnp.concatenate(cols, axis=1) + bs_ref[...])
    s_out = u * jnp.concatenate(mixed_rows, axis=0)

    a = za * jax.nn.sigmoid(zb)
    a_scr[halo:halo + tm, :] = a
    base = halo - (taps - 1)
    dc = jnp.zeros((tm, width), F32) + db_ref[...]
    for s in range(SUBLANES):
        ks = list(range(s, taps, SUBLANES))
        rows = tm + SUBLANES * (len(ks) - 1)
        sh_scr[0:rows, :] = a_scr[base + s:base + s + rows, :]
        for n, k in enumerate(ks):
            dc = dc + dw_ref[k:k + 1, :] * sh_scr[SUBLANES * n:SUBLANES * n + tm, :]
    cd_ref[...] = a_scr[halo + tm - (taps - 1):halo + tm, :]
    a_scr[0:halo, :] = a_scr[tm:tm + halo, :]
    d_out = _silu(_layer_norm(dc, dlg_ref[...], dlb_ref[...]))

    o = jnp.dot(s_out.astype(BF16), wout_ref[0:width, :], preferred_element_type=F32)
    o = o + jnp.dot(d_out.astype(BF16), wout_ref[width:2 * width, :], preferred_element_type=F32)
    y_ref[...] = x + gate * o


def _mixer_prompt(x, norm_g, mod, win_bf, c_ln_g, c_ln_b, ws, bs, dw, db, d_ln_g, d_ln_b, wout_bf,
                  *, batch, seq_len):
    m, d = x.shape
    width = win_bf.shape[1] // 4
    tm = TM_MIX
    tps = seq_len // tm
    gw = width // C_GROUPS
    bs_exp = jnp.repeat(bs.T, gw, axis=1)
    row = lambda a: a.reshape(1, width)
    return pl.pallas_call(
        functools.partial(_mixer_kernel, tiles_per_seq=tps, tm=tm, width=width),
        out_shape=(jax.ShapeDtypeStruct((m, d), F32),
                   jax.ShapeDtypeStruct((batch, CHUNK, width), F32),
                   jax.ShapeDtypeStruct((batch, D_CONV - 1, width), F32)),
        grid=(m // tm,),
        in_specs=[
            pl.BlockSpec((tm, d), lambda i: (i, 0)),
            _resident((1, d)),
            _resident(mod.shape),
            _resident(win_bf.shape),
            _resident((1, width)), _resident((1, width)),
            _resident(ws.shape),
            _resident(bs_exp.shape),
            _resident(dw.shape),
            _resident((1, width)), _resident((1, width)), _resident((1, width)),
            _resident(wout_bf.shape),
        ],
        out_specs=(pl.BlockSpec((tm, d), lambda i: (i, 0)),
                   pl.BlockSpec((None, CHUNK, width), lambda i: (i // tps, 0, 0)),
                   pl.BlockSpec((None, D_CONV - 1, width), lambda i: (i // tps, 0, 0))),
        scratch_shapes=[pltpu.VMEM((tm + 4 * SUBLANES, width), F32),
                        pltpu.VMEM((tm + 3 * SUBLANES, width), F32)],
        compiler_params=_cparams(("arbitrary",)),
        name="mixer",
    )(x, norm_g.reshape(1, d), mod, win_bf, row(c_ln_g), row(c_ln_b), ws, bs_exp, dw, row(db),
      row(d_ln_g), row(d_ln_b), wout_bf)


def _mixer_dec_kernel(x_ref, g_ref, mod_ref, win_ref, clg_ref, clb_ref, w0_ref, b0_ref, past_ref,
                      dw_ref, db_ref, dlg_ref, dlb_ref, wout_ref, y_ref, v_ref, a_ref, *, width):
    x = x_ref[...]
    h = _norm_mod(x, g_ref[...], mod_ref[1], mod_ref[0]).astype(BF16)
    u = _gelu_tanh(jnp.dot(h, win_ref[:, 0:width], preferred_element_type=F32))
    v = _gelu_tanh(jnp.dot(h, win_ref[:, width:2 * width], preferred_element_type=F32))
    v = _layer_norm(v, clg_ref[...], clb_ref[...])
    v_ref[...] = v
    mixed = w0_ref[...] * v + b0_ref[...]
    s_out = u * mixed
    za = jnp.dot(h, win_ref[:, 2 * width:3 * width], preferred_element_type=F32)
    zb = jnp.dot(h, win_ref[:, 3 * width:4 * width], preferred_element_type=F32)
    a = za * jax.nn.sigmoid(zb)
    a_ref[...] = a
    dc = dw_ref[D_CONV - 1:D_CONV, :] * a + db_ref[...]
    for k in range(D_CONV - 1):
        dc = dc + dw_ref[k:k + 1, :] * past_ref[k]
    d_out = _silu(_layer_norm(dc, dlg_ref[...], dlb_ref[...]))
    o = jnp.dot(s_out.astype(BF16), wout_ref[0:width, :], preferred_element_type=F32)
    o = o + jnp.dot(d_out.astype(BF16), wout_ref[width:2 * width, :], preferred_element_type=F32)
    y_ref[...] = x + mod_ref[2] * o


def _mixer_decode(x, norm_g, mod, win_bf, c_ln_g, c_ln_b, ws, bs, past_t, dw, db, d_ln_g, d_ln_b, wout_bf):
    m, d = x.shape
    width = win_bf.shape[1] // 4
    gw = width // C_GROUPS
    row = lambda a: a.reshape(1, width)
    w0 = jnp.repeat(ws[:, 0, 0], gw).reshape(1, width)
    b0 = jnp.repeat(bs[:, 0], gw).reshape(1, width)
    args = (x, norm_g.reshape(1, d), mod, win_bf, row(c_ln_g), row(c_ln_b), w0, b0, past_t, dw, row(db),
            row(d_ln_g), row(d_ln_b), wout_bf)
    full = lambda a: pl.BlockSpec(a.shape, lambda i, nd=a.ndim: (0,) * nd)
    return pl.pallas_call(
        functools.partial(_mixer_dec_kernel, width=width),
        out_shape=(jax.ShapeDtypeStruct((m, d), F32),
                   jax.ShapeDtypeStruct((m, width), F32),
                   jax.ShapeDtypeStruct((m, width), F32)),
        grid=(1,),
        in_specs=[full(a) for a in args],
        out_specs=(pl.BlockSpec((m, d), lambda i: (0, 0)),
                   pl.BlockSpec((m, width), lambda i: (0, 0)),
                   pl.BlockSpec((m, width), lambda i: (0, 0))),
        compiler_params=_cparams(("arbitrary",)),
        name="mixer_decode",
    )(*args)


def kernel(x_prompt, x_sample, c_prompt, c_sample, cache_k, cache_v, state_conv_d, state_ffn_conv,
           page_table, norm_g, ada_w, ada_b, att_w_in, att_lam, att_sub_g, att_w_out, mix_w_in,
           c_ln_g, c_ln_b, c_ws, c_bs, d_conv_w, d_conv_b, d_ln_g, d_ln_b, mix_w_out,
           ffn_w_in, ffn_conv_w, ffn_conv_b, ffn_w_out, final_g):
    batch, seq_len, d = x_prompt.shape
    bs = x_sample.shape[0]
    depth = ada_w.shape[0]
    att_width = att_w_out.shape[1]
    n_heads = att_width // HEAD_DIM
    hidden = ffn_w_out.shape[1]
    assert x_sample.shape[1] == 1 and seq_len % TQ == 0 and seq_len % TM_PROJ == 0
    assert bs % (2 * SUBLANES) == 0 and page_table.shape[1] % PAGES_PER_STEP == 0

    rows = bs + batch
    pad = (-rows) % (2 * SUBLANES)
    c_all = jnp.concatenate([c_sample, c_prompt, jnp.zeros((pad, d), F32)], axis=0)
    mod = _ada_mod(c_all, ada_w, ada_b)
    mod_s = mod[:, :, 0:bs]
    mod_p = mod[:, :, bs:bs + batch]

    xp = x_prompt.reshape(batch * seq_len, d)
    xs = x_sample.reshape(bs, d)
    k_s, v_s, cv_p, cv_s, cd_p, cd_s, ff_p, ff_s = ([] for _ in range(8))
    kt_p = vt_p = None

    for i in range(depth):
        j = i // 2
        if i % 2 == 0:
            lam_init = 0.8 - 0.6 * math.exp(-0.3 * i)
            w_in = att_w_in[j].astype(BF16)
            w_out = att_w_out[j].astype(BF16)
            n_sb = N_SB_HEADS * HEAD_DIM
            wq = w_in[:, 0:att_width]
            wkt = w_in[:, att_width:2 * att_width].T
            wvt = w_in[:, 2 * att_width:3 * att_width].T
            q, kt_p, vt_p, ktb, vtb = _qkv_prompt(xp, norm_g[i, 0], mod_p[i], wq, wkt, wvt, kt_p, vt_p,
                                                  batch=batch, seq_len=seq_len, tm=TQ)
            q_s, k, v = _qkv_decode(xs, norm_g[i, 0], mod_s[i], w_in)
            if _fused_steps_match(batch, seq_len, TQ, bs, page_table.shape[1]):
                o_sb, o = _attn_sb_decode(q, ktb, vtb, q_s, k, v, cache_k, cache_v, page_table, att_lam[j],
                                          att_sub_g[j], batch=batch, seq_len=seq_len, tq=TQ, layer=j,
                                          lam_init=lam_init)
            else:
                o_sb = _attn_sb(q, ktb, vtb, batch=batch, seq_len=seq_len, tq=TQ)
                o = _decode_attn(q_s, k, v, cache_k, cache_v, page_table, att_lam[j], att_sub_g[j],
                                 layer=j, lam_init=lam_init)
            o_d = _attn_diff(q, ktb, vtb, att_lam[j], att_sub_g[j], batch=batch, seq_len=seq_len, tq=TQ,
                             lam_init=lam_init)
            xp = _out_proj(xp, o_sb, o_d, mod_p[i], w_out, tm=TM_PROJ,
                           tiles_per_seq=seq_len // TM_PROJ, per_row=False)
            xs = _out_proj(xs, o[:, 0:n_sb], o[:, n_sb:], mod_s[i], w_out, tm=bs, tiles_per_seq=1,
                           per_row=True)
            k_s.append(k.reshape(bs, 1, n_heads, HEAD_DIM))
            v_s.append(v.reshape(bs, 1, n_heads, HEAD_DIM))
        else:
            w_in = mix_w_in[j].astype(BF16)
            w_out = mix_w_out[j].astype(BF16)
            xp, cv, cd = _mixer_prompt(xp, norm_g[i, 0], mod_p[i], w_in, c_ln_g[j], c_ln_b[j], c_ws[j],
                                       c_bs[j], d_conv_w[j], d_conv_b[j], d_ln_g[j], d_ln_b[j], w_out,
                                       batch=batch, seq_len=seq_len)
            cv_p.append(cv)
            cd_p.append(cd)
            past = state_conv_d[j]
            xs, v_new, a_new = _mixer_decode(xs, norm_g[i, 0], mod_s[i], w_in, c_ln_g[j], c_ln_b[j],
                                             c_ws[j], c_bs[j], past.transpose(1, 0, 2), d_conv_w[j],
                                             d_conv_b[j], d_ln_g[j], d_ln_b[j], w_out)
            cv_s.append(v_new[:, None, :])
            cd_s.append(jnp.concatenate([past[:, 1:], a_new[:, None, :]], axis=1))
        final = i == depth - 1
        w_in = ffn_w_in[i].astype(BF16)
        w_out = ffn_w_out[i].astype(BF16)
        xp, ff = _ffn_prompt(xp, norm_g[i, 1], mod_p[i], w_in, ffn_conv_w[i], ffn_conv_b[i], w_out,
                             final_g, batch=batch, seq_len=seq_len, final=final)
        ff_p.append(ff)
        past = state_ffn_conv[i]
        xs, g_new = _ffn_decode(xs, norm_g[i, 1], mod_s[i], w_in, past.transpose(1, 0, 2), ffn_conv_w[i],
                                ffn_conv_b[i], w_out, final_g, final=final)
        ff_s.append(jnp.concatenate([past[:, 1:], g_new[:, None, :]], axis=1))

    def to_cache_layout(t):
        return t.reshape(t.shape[0], batch, n_heads, HEAD_DIM, seq_len).transpose(0, 1, 4, 2, 3)

    return (xp.reshape(batch, seq_len, d), xs.reshape(bs, 1, d),
            to_cache_layout(kt_p), to_cache_layout(vt_p), jnp.stack(k_s), jnp.stack(v_s),
            jnp.stack(cv_p), jnp.stack(cv_s), jnp.stack(cd_p), jnp.stack(cd_s),
            jnp.stack(ff_p), jnp.stack(ff_s))
```

```python
import functools
import math

import jax
import jax.numpy as jnp
from jax import lax
from jax.experimental import pallas as pl
from jax.experimental.pallas import tpu as pltpu

F32 = jnp.float32
BF16 = jnp.bfloat16

HEAD_DIM = 64
N_SB_HEADS = 8
N_DIFF_HEADS = 4
N_MOD = 6
C_GROUPS = 4
CHUNK = 128
D_CONV = 31
FFN_CONV = 3
EPS = 1e-6
NEG_INF = -1e30
SB_DEAD = 110.0

LANES = 128
SUBLANES = 8
VMEM_LIMIT = 56 * 1024 * 1024

TM_PROJ = 512
TM_FFN = 256
TM_MIX = 256
TQ = 256
PAGES_PER_STEP = 16
FFN_CHUNK = 256
FFN_CHUNK_DEC = 1408


def _cparams(sem, vmem=VMEM_LIMIT):
    return pltpu.CompilerParams(dimension_semantics=sem, vmem_limit_bytes=vmem)


def _resident(shape):
    nd = len(shape)
    return pl.BlockSpec(shape, lambda *_: (0,) * nd, pipeline_mode=pl.Buffered(1))


def _norm_mod(x, g, sc, sh):
    ms = jnp.mean(x * x, axis=-1, keepdims=True)
    y = x * lax.rsqrt(ms + EPS) * g
    return y * (1.0 + sc) + sh


def _mod_row(mod_ref, slot, seq, per_row):
    if per_row:
        return mod_ref[slot]
    return mod_ref[slot, pl.ds(seq, 1), :]


def _layer_norm(x, g, b):
    mu = jnp.mean(x, axis=-1, keepdims=True)
    xc = x - mu
    var = jnp.mean(xc * xc, axis=-1, keepdims=True)
    return xc * lax.rsqrt(var + EPS) * g + b


def _gelu_tanh(x):
    c = math.sqrt(2.0 / math.pi)
    return x * (0.5 * (1.0 + jnp.tanh(c * (x + 0.044715 * (x * x * x)))))


def _silu(x):
    return x * jax.nn.sigmoid(x)


def _softplus(z):
    return jnp.maximum(z, 0.0) + jnp.log(1.0 + jnp.exp(-jnp.abs(z)))


def _split_bf16(x):
    hi = x.astype(BF16)
    lo = (x - hi.astype(F32)).astype(BF16)
    return hi, lo


def _ada_kernel(c_ref, w_ref, b_ref, o_ref):
    c = c_ref[...]
    s = _silu(c).astype(BF16)
    w = w_ref[...].astype(BF16)
    o_ref[...] = jnp.dot(s, w, preferred_element_type=F32) + b_ref[...]


def _ada_mod(c_all, ada_w, ada_b):
    depth, d, _ = ada_w.shape
    rows = c_all.shape[0]
    return pl.pallas_call(
        _ada_kernel,
        out_shape=jax.ShapeDtypeStruct((depth, N_MOD, rows, d), F32),
        grid=(depth, N_MOD),
        in_specs=[
            pl.BlockSpec((rows, d), lambda i, j: (0, 0)),
            pl.BlockSpec((None, d, d), lambda i, j: (i, 0, j)),
            pl.BlockSpec((None, None, 1, d), lambda i, j: (i, j, 0, 0)),
        ],
        out_specs=pl.BlockSpec((None, None, rows, d), lambda i, j: (i, j, 0, 0)),
        compiler_params=_cparams(("parallel", "parallel")),
        name="ada_mod",
    )(c_all, ada_w, ada_b.reshape(depth, N_MOD, 1, d))


def _qkv_dec_kernel(x_ref, g_ref, mod_ref, w_ref, q_ref, k_ref, v_ref, *, width):
    h = _norm_mod(x_ref[...], g_ref[...], mod_ref[1], mod_ref[0]).astype(BF16)
    q = jnp.dot(h, w_ref[:, 0:width], preferred_element_type=F32)
    q_ref[...] = (q * (HEAD_DIM ** -0.5)).astype(BF16)
    k_ref[...] = jnp.dot(h, w_ref[:, width:2 * width], preferred_element_type=F32)
    v_ref[...] = jnp.dot(h, w_ref[:, 2 * width:3 * width], preferred_element_type=F32)


def _qkv_decode(x, norm_g, mod, w_bf):
    m, d = x.shape
    width = w_bf.shape[1] // 3
    tile = lambda dt: jax.ShapeDtypeStruct((m, width), dt)
    full = lambda shape: pl.BlockSpec(shape, lambda i: (0,) * len(shape))
    return pl.pallas_call(
        functools.partial(_qkv_dec_kernel, width=width),
        out_shape=(tile(BF16), tile(F32), tile(F32)),
        grid=(1,),
        in_specs=[full((m, d)), full((1, d)), full(mod.shape), full(w_bf.shape)],
        out_specs=(full((m, width)),) * 3,
        compiler_params=_cparams(("arbitrary",)),
        name="qkv_decode",
    )(x, norm_g.reshape(1, d), mod, w_bf)


def _qkv_t_kernel(x_ref, g_ref, mod_ref, wq_ref, wkt_ref, wvt_ref, *rest, tiles_per_seq, n_prev):
    if n_prev:
        pk_ref, pv_ref = rest[0:2]
        rest = rest[2:]
    q_ref, kt_ref, vt_ref, ktb_ref, vtb_ref = rest
    seq = pl.program_id(0) // tiles_per_seq
    sh = _mod_row(mod_ref, 0, seq, False)
    sc = _mod_row(mod_ref, 1, seq, False)
    h = _norm_mod(x_ref[...], g_ref[...], sc, sh).astype(BF16)
    q = jnp.dot(h, wq_ref[...], preferred_element_type=F32)
    q_ref[...] = (q * (HEAD_DIM ** -0.5)).astype(BF16)
    nt = (((1,), (1,)), ((), ()))
    kt = lax.dot_general(wkt_ref[...], h, nt, preferred_element_type=F32)
    vt = lax.dot_general(wvt_ref[...], h, nt, preferred_element_type=F32)
    if n_prev:
        kt_ref[0:n_prev] = pk_ref[...]
        vt_ref[0:n_prev] = pv_ref[...]
    kt_ref[n_prev] = kt
    vt_ref[n_prev] = vt
    ktb_ref[...] = kt.astype(BF16)
    vtb_ref[...] = vt.astype(BF16)


def _qkv_prompt(x, norm_g, mod, wq, wkt, wvt, prev_k, prev_v, *, batch, seq_len, tm):
    m, d = x.shape
    width = wq.shape[1]
    tps = seq_len // tm
    n_prev = 0 if prev_k is None else prev_k.shape[0]
    slab = lambda n: pl.BlockSpec((n, None, width, tm), lambda i: (0, i // tps, 0, i % tps))
    blocked = pl.BlockSpec((None, None, width, tm), lambda i: (i // tps, i % tps, 0, 0))
    in_specs = [
        pl.BlockSpec((tm, d), lambda i: (i, 0)),
        _resident((1, d)),
        _resident(mod.shape),
        _resident(wq.shape), _resident(wkt.shape), _resident(wvt.shape),
    ]
    args = [x, norm_g.reshape(1, d), mod, wq, wkt, wvt]
    if n_prev:
        in_specs += [slab(n_prev), slab(n_prev)]
        args += [prev_k, prev_v]
    stacked = jax.ShapeDtypeStruct((n_prev + 1, batch, width, seq_len), F32)
    blk = jax.ShapeDtypeStruct((batch, tps, width, tm), BF16)
    return pl.pallas_call(
        functools.partial(_qkv_t_kernel, tiles_per_seq=tps, n_prev=n_prev),
        out_shape=(jax.ShapeDtypeStruct((m, width), BF16), stacked, stacked, blk, blk),
        grid=(m // tm,),
        in_specs=in_specs,
        out_specs=(pl.BlockSpec((tm, width), lambda i: (i, 0)), slab(n_prev + 1), slab(n_prev + 1),
                   blocked, blocked),
        compiler_params=_cparams(("parallel",)),
        name="qkv_prompt",
    )(*args)


def _oproj_kernel(x_ref, oa_ref, ob_ref, mod_ref, w_ref, y_ref, *, tiles_per_seq, per_row, split):
    seq = pl.program_id(0) // tiles_per_seq
    gate = _mod_row(mod_ref, 2, seq, per_row)
    o = jnp.dot(oa_ref[...], w_ref[0:split, :], preferred_element_type=F32)
    o = o + jnp.dot(ob_ref[...], w_ref[split:, :], preferred_element_type=F32)
    y_ref[...] = x_ref[...] + gate * o


def _out_proj(x, oa, ob, mod, w_bf, *, tm, tiles_per_seq, per_row):
    m, d = x.shape
    split = oa.shape[1]
    return pl.pallas_call(
        functools.partial(_oproj_kernel, tiles_per_seq=tiles_per_seq, per_row=per_row, split=split),
        out_shape=jax.ShapeDtypeStruct((m, d), F32),
        grid=(m // tm,),
        in_specs=[
            pl.BlockSpec((tm, d), lambda i: (i, 0)),
            pl.BlockSpec((tm, split), lambda i: (i, 0)),
            pl.BlockSpec((tm, ob.shape[1]), lambda i: (i, 0)),
            _resident(mod.shape),
            _resident(w_bf.shape),
        ],
        out_specs=pl.BlockSpec((tm, d), lambda i: (i, 0)),
        compiler_params=_cparams(("parallel",)),
        name="out_proj",
    )(x, oa, ob, mod, w_bf)


_NT = (((1,), (1,)), ((), ()))


def _sb_kernel(q_ref, kt_ref, vt_ref, uu_ref, o_ref, *, tq):
    _sb_body(pl.program_id(2), q_ref, kt_ref, vt_ref, uu_ref, o_ref, tq)


def _sb_body(i, q_ref, kt_ref, vt_ref, uu_ref, o_ref, tq):
    q = q_ref[...]
    lane = lax.broadcasted_iota(jnp.int32, (1, LANES), 1)
    row = lax.broadcasted_iota(jnp.int32, (tq, tq), 0)
    col = lax.broadcasted_iota(jnp.int32, (tq, tq), 1)
    strict = col < row
    uu = uu_ref[...]
    zero_q = jnp.zeros_like(q)
    qs = (jnp.where(lane < HEAD_DIM, q, zero_q), jnp.where(lane >= HEAD_DIM, q, zero_q))

    def blocks(js, state, diag):
        kbs = [kt_ref[j] for j in js]
        vbs = [vt_ref[j] for j in js]
        chains = [(n, h) for n in range(len(js)) for h in range(2)]
        z = {c: jnp.dot(qs[c[1]], kbs[c[0]], preferred_element_type=F32) for c in chains}
        sp = {c: _softplus(z[c]) for c in chains}
        lk = {c: jnp.where(strict, sp[c], 0.0) if (diag and c[0] == 0) else sp[c] for c in chains}
        cs = {c: jnp.dot(jnp.concatenate(_split_bf16(lk[c]), axis=1), uu, preferred_element_type=F32)
              for c in chains}
        tot = {c: jnp.sum(lk[c], axis=1, keepdims=True) for c in chains}
        new = []
        for h in range(2):
            carry, acc = state[2 * h], state[2 * h + 1]
            for n in range(len(js)):
                c = (n, h)
                a = jnp.exp(z[c] - sp[c] - cs[c] - carry)
                if diag and n == 0:
                    a = jnp.where(strict, a, 0.0)
                acc = acc + lax.dot_general(a.astype(BF16), vbs[n], _NT, preferred_element_type=F32)
                carry = carry + tot[c]
            new += [carry, acc]
        return tuple(new)

    c0 = jnp.zeros((tq, 1), F32)
    a0 = jnp.zeros((tq, LANES), F32)
    state = (c0, a0, c0, a0)
    has_prev = jnp.minimum(i, 1)
    state = lax.fori_loop(0, 1 - has_prev, lambda t, st: blocks([i], st, True), state)
    state = lax.fori_loop(0, has_prev, lambda t, st: blocks([i, i - 1], st, True), state)
    rest = jnp.maximum(i - 1, 0)

    def live(st):
        return jnp.min(jnp.minimum(st[0], st[2])) < SB_DEAD

    def pair_cond(c):
        return jnp.logical_and(c[0] < rest // 2, live(c[1]))

    def pair_body(c):
        t, st = c
        return t + 1, blocks([i - 2 - 2 * t, i - 3 - 2 * t], st, False)

    _, state = lax.while_loop(pair_cond, pair_body, (jnp.int32(0), state))
    last = jnp.where(live(state), rest % 2, 0)
    state = lax.fori_loop(0, last, lambda t, st: blocks([0], st, False), state)
    o_ref[...] = jnp.where(lane < HEAD_DIM, state[1], state[3]).astype(o_ref.dtype)


def _suffix_matrix(n):
    j = lax.broadcasted_iota(jnp.int32, (n, n), 0)
    s = lax.broadcasted_iota(jnp.int32, (n, n), 1)
    u = (j > s).astype(BF16)
    return jnp.concatenate([u, u], axis=0)


def _attn_sb(q_bf, ktb, vtb, *, batch, seq_len, tq):
    m = q_bf.shape[0]
    nq = seq_len // tq
    n_pairs = N_SB_HEADS * HEAD_DIM // LANES
    kv_spec = pl.BlockSpec((None, nq, LANES, tq), lambda b, p, i: (b, 0, p, 0))
    return pl.pallas_call(
        functools.partial(_sb_kernel, tq=tq),
        out_shape=jax.ShapeDtypeStruct((m, n_pairs * LANES), BF16),
        grid=(batch, n_pairs, nq),
        in_specs=[
            pl.BlockSpec((tq, LANES), lambda b, p, i: (b * nq + i, p)),
            kv_spec, kv_spec,
            _resident((2 * tq, tq)),
        ],
        out_specs=pl.BlockSpec((tq, LANES), lambda b, p, i: (b * nq + i, p)),
        compiler_params=_cparams(("parallel", "parallel", "parallel")),
        name="attn_sb",
    )(q_bf, ktb, vtb, _suffix_matrix(tq))


def _lam_value(lam_ref, lam_init):
    lp = lam_ref[...]
    t1 = jnp.sum(lp[0:1] * lp[1:2], axis=1, keepdims=True)
    t2 = jnp.sum(lp[2:3] * lp[3:4], axis=1, keepdims=True)
    return jnp.exp(t1) - jnp.exp(t2) + lam_init


def _diff_kernel(lam_ref, g_ref, q_ref, kt_ref, vt_ref, o_ref, *, tq, lam_init):
    _diff_body(pl.program_id(2), lam_ref, g_ref, q_ref, kt_ref, vt_ref, o_ref, tq, lam_init)


def _diff_body(i, lam_ref, g_ref, q_ref, kt_ref, vt_ref, o_ref, tq, lam_init):
    q = q_ref[...]
    lane = lax.broadcasted_iota(jnp.int32, (1, LANES), 1)
    row = lax.broadcasted_iota(jnp.int32, (tq, 2 * tq), 0)
    col = lax.broadcasted_iota(jnp.int32, (tq, 2 * tq), 1)
    causal = jnp.logical_or(col <= row, col >= tq)
    zero_q = jnp.zeros_like(q)
    qs = (jnp.where(lane < HEAD_DIM, q, zero_q), jnp.where(lane >= HEAD_DIM, q, zero_q))

    def blocks(js, state, diag):
        kb = jnp.concatenate([kt_ref[j] for j in js], axis=1)
        vb = jnp.concatenate([vt_ref[j] for j in js], axis=1)
        s = [jnp.dot(qs[c], kb, preferred_element_type=F32) for c in range(2)]
        new = []
        for c in range(2):
            m_old, l, acc = state[3 * c:3 * c + 3]
            sc = jnp.where(causal[:, 0:len(js) * tq], s[c], NEG_INF) if diag else s[c]
            m_new = jnp.maximum(m_old, jnp.max(sc, axis=1, keepdims=True))
            alpha = jnp.exp(m_old - m_new)
            p = jnp.exp(sc - m_new)
            l = alpha * l + jnp.sum(p, axis=1, keepdims=True)
            acc = alpha * acc + lax.dot_general(p.astype(BF16), vb, _NT, preferred_element_type=F32)
            new += [m_new, l, acc]
        return tuple(new)

    m0 = jnp.full((tq, 1), NEG_INF, F32)
    l0 = jnp.zeros((tq, 1), F32)
    a0 = jnp.zeros((tq, LANES), F32)
    state = (m0, l0, a0, m0, l0, a0)
    has_prev = jnp.minimum(i, 1)
    state = lax.fori_loop(0, 1 - has_prev, lambda t, st: blocks([i], st, True), state)
    state = lax.fori_loop(0, has_prev, lambda t, st: blocks([i, i - 1], st, True), state)
    rest = jnp.maximum(i - 1, 0)
    state = lax.fori_loop(0, rest // 2, lambda t, st: blocks([i - 2 - 2 * t, i - 3 - 2 * t], st, False), state)
    state = lax.fori_loop(0, rest % 2, lambda t, st: blocks([0], st, False), state)

    lam = _lam_value(lam_ref, lam_init)
    o = state[2] / state[1] - lam * (state[5] / state[4])
    ms = jnp.mean(o * o, axis=1, keepdims=True)
    o = o * lax.rsqrt(ms + EPS) * g_ref[...] * (1.0 - lam_init)
    o_ref[...] = o.astype(o_ref.dtype)


def _attn_diff(q_bf, ktb, vtb, lam_p, sub_g, *, batch, seq_len, tq, lam_init):
    m = q_bf.shape[0]
    nq = seq_len // tq
    off = N_SB_HEADS * HEAD_DIM // LANES
    kv_spec = pl.BlockSpec((None, nq, LANES, tq), lambda b, p, i: (b, 0, off + p, 0))
    return pl.pallas_call(
        functools.partial(_diff_kernel, tq=tq, lam_init=lam_init),
        out_shape=jax.ShapeDtypeStruct((m, N_DIFF_HEADS * LANES), BF16),
        grid=(batch, N_DIFF_HEADS, nq),
        in_specs=[
            _resident(lam_p.shape),
            _resident((1, LANES)),
            pl.BlockSpec((tq, LANES), lambda b, p, i: (b * nq + i, off + p)),
            kv_spec, kv_spec,
        ],
        out_specs=pl.BlockSpec((tq, LANES), lambda b, p, i: (b * nq + i, p)),
        compiler_params=_cparams(("parallel", "parallel", "parallel")),
        name="attn_diff",
    )(lam_p, sub_g.reshape(1, LANES), q_bf, ktb, vtb)


def _dec_attn_kernel(pt_ref, *refs, n_steps, pages, n_heads, lam_init):
    del pt_ref
    _dec_body(pl.program_id(0), pl.program_id(1), refs[0:7 + 2 * pages], refs[7 + 2 * pages],
              refs[8 + 2 * pages:], n_steps=n_steps, pages=pages, n_heads=n_heads, lam_init=lam_init)


def _dec_body(seq, t, in_refs, o_ref, scratch, *, n_steps, pages, n_heads, lam_init):
    qt_ref, kx_ref, vx_ref, lam_ref, g_ref, uo_ref, msuf_ref = in_refs[0:7]
    k_refs = in_refs[7:7 + pages]
    v_refs = in_refs[7 + pages:7 + 2 * pages]
    s_scr, w_scr, x_scr, acc_scr, qb_scr = scratch
    n_pages = n_steps * pages
    n_rows = n_pages * n_heads
    lane = lax.broadcasted_iota(jnp.int32, (1, LANES), 1)
    rid = lax.broadcasted_iota(jnp.int32, (n_heads, 1), 0)
    is_sb = rid < N_SB_HEADS

    def value_row(wp, h):
        if h < N_SB_HEADS:
            return wp[h:h + 1, :]
        first = N_SB_HEADS + 2 * ((h - N_SB_HEADS) // 2)
        return wp[first:first + 1, :] + wp[first + 1:first + 2, :]

    def q_head(h):
        return qb_scr[h * HEAD_DIM:(h + 1) * HEAD_DIM, :]

    @pl.when(t == 0)
    def _():
        acc_scr[...] = jnp.zeros_like(acc_scr)
        src = lax.broadcasted_iota(jnp.int32, (LANES, LANES), 0)
        pick = jnp.where(src == seq, 1.0, 0.0).astype(BF16)
        qb_scr[...] = jnp.dot(qt_ref[...], pick, preferred_element_type=F32)

    @pl.when(t < n_steps)
    def _():
        for h in range(n_heads):
            qh = q_head(h)
            for r in range(pages):
                row = jnp.sum(k_refs[r][h] * qh, axis=0, keepdims=True)
                s_scr[t * pages + r, pl.ds(h, 1), :] = row

    @pl.when(t == n_steps - 1)
    def _():
        for h in range(n_heads):
            x_scr[pl.ds(h, 1), :] = jnp.sum(kx_ref[h] * q_head(h), axis=0, keepdims=True)
        s3 = s_scr[...]
        s = s3.reshape(n_rows, LANES)
        sp = _softplus(s)
        hi, lo = _split_bf16(sp)
        r1 = jnp.dot(jnp.concatenate([hi, lo], axis=1), uo_ref[...], preferred_element_type=F32)
        cs = r1[:, 0:LANES]
        tot = r1[:, LANES:2 * LANES]
        t1 = tot.astype(BF16)
        rem = tot - t1.astype(F32)
        t2, t3 = _split_bf16(rem)
        c3 = jnp.dot(msuf_ref[...], jnp.concatenate([t1, t2, t3], axis=1), preferred_element_type=F32)
        carry = c3[:, 0:LANES] + c3[:, LANES:2 * LANES] + c3[:, 2 * LANES:3 * LANES]
        a3 = jnp.exp(s - sp - cs - carry).reshape(n_pages, n_heads, LANES)
        sx = jnp.where(lane == seq, x_scr[...], NEG_INF)
        m = jnp.max(jnp.maximum(jnp.max(s3, axis=0), sx), axis=1, keepdims=True)
        p3 = jnp.exp(s3 - m[None])
        px = jnp.exp(sx - m)
        l = jnp.sum(jnp.sum(p3, axis=0) + px, axis=1, keepdims=True)
        lam = _lam_value(lam_ref, lam_init)
        coef = jnp.where(lax.bitwise_and(rid, 1) == 0, 1.0, -lam) / l
        w_scr[...] = jnp.where(is_sb[None], a3, p3 * coef[None])
        x_scr[...] = jnp.where(is_sb, 0.0, px * coef)

    @pl.when(t >= n_steps)
    def _():
        blk = t - n_steps
        wps = [w_scr[blk * pages + r] for r in range(pages)]
        for h in range(n_heads):
            acc = acc_scr[h]
            for r in range(pages):
                acc = acc + value_row(wps[r], h) * v_refs[r][h]
            acc_scr[h] = acc

    @pl.when(t == 2 * n_steps - 1)
    def _():
        wx = x_scr[...]
        for h in range(n_heads):
            acc_scr[h] = acc_scr[h] + value_row(wx, h) * vx_ref[h]
        hi, lo = _split_bf16(acc_scr[...].reshape(n_heads * HEAD_DIM, LANES))
        ones = jnp.ones((SUBLANES, LANES), BF16)
        nt = (((1,), (1,)), ((), ()))
        o = lax.dot_general(ones, hi, nt, preferred_element_type=F32)
        o = (o + lax.dot_general(ones, lo, nt, preferred_element_type=F32))[0:1, :]
        n_sb = N_SB_HEADS * HEAD_DIM
        o_ref[:, 0:n_sb] = o[:, 0:n_sb].astype(o_ref.dtype)
        for hd in range(N_DIFF_HEADS):
            od = o[:, n_sb + hd * LANES:n_sb + (hd + 1) * LANES]
            ms = jnp.mean(od * od, axis=1, keepdims=True)
            od = od * lax.rsqrt(ms + EPS) * g_ref[...] * (1.0 - lam_init)
            o_ref[:, n_sb + hd * LANES:n_sb + (hd + 1) * LANES] = od.astype(o_ref.dtype)


def _suffix_and_total_matrix():
    j = lax.broadcasted_iota(jnp.int32, (LANES, LANES), 0)
    s = lax.broadcasted_iota(jnp.int32, (LANES, LANES), 1)
    u1 = jnp.concatenate([(j > s).astype(BF16), jnp.ones((LANES, LANES), BF16)], axis=1)
    return jnp.concatenate([u1, u1], axis=0)


def _later_pages_matrix(n_pages, n_heads):
    n = n_pages * n_heads
    a = lax.broadcasted_iota(jnp.int32, (n, n), 0)
    b = lax.broadcasted_iota(jnp.int32, (n, n), 1)
    same_head = (a % n_heads) == (b % n_heads)
    return jnp.logical_and(same_head, b // n_heads > a // n_heads).astype(BF16)


def _decode_operands(q_bf, k_new, v_new, cache_k, cache_v, page_table, lam_p, sub_g, *, layer, seq_step):
    bs, width = q_bf.shape
    page, n_heads = cache_k.shape[2], cache_k.shape[3]
    assert page == LANES and cache_k.shape[4] == HEAD_DIM and bs <= LANES
    n_pages = page_table.shape[1]
    pages = PAGES_PER_STEP
    n_steps = n_pages // pages
    ck = cache_k.transpose(0, 1, 3, 4, 2)
    cv = cache_v.transpose(0, 1, 3, 4, 2)
    lanes = lambda a: jnp.pad(a.T, ((0, 0), (0, LANES - bs)))
    qt = lanes(q_bf)
    kx = lanes(k_new).reshape(n_heads, HEAD_DIM, LANES)
    vx = lanes(v_new).reshape(n_heads, HEAD_DIM, LANES)

    def k_map(r):
        def index(*ids):
            seq, t = seq_step(*ids[:-1])
            return (layer, ids[-1][seq, jnp.minimum(t, n_steps - 1) * pages + r], 0, 0, 0)
        return index

    def v_map(r):
        def index(*ids):
            seq, t = seq_step(*ids[:-1])
            return (layer, ids[-1][seq, jnp.maximum(t - n_steps, 0) * pages + r], 0, 0, 0)
        return index

    const = lambda shape: pl.BlockSpec(shape, lambda *ids: (0,) * len(shape))
    page_block = (None, None, n_heads, HEAD_DIM, LANES)
    n_rows = n_pages * n_heads
    in_specs = [const(qt.shape), const(kx.shape), const(vx.shape),
                const(lam_p.shape), const((1, LANES)), const((2 * LANES, 2 * LANES)),
                const((n_rows, n_rows))]
    in_specs += [pl.BlockSpec(page_block, k_map(r)) for r in range(pages)]
    in_specs += [pl.BlockSpec(page_block, v_map(r)) for r in range(pages)]
    args = [qt, kx, vx, lam_p, sub_g.reshape(1, LANES), _suffix_and_total_matrix(),
            _later_pages_matrix(n_pages, n_heads)] + [ck] * pages + [cv] * pages
    out_spec = pl.BlockSpec((None, 1, width), lambda *ids: (seq_step(*ids[:-1])[0], 0, 0))
    scratch = [
        pltpu.VMEM((n_pages, n_heads, LANES), F32),
        pltpu.VMEM((n_pages, n_heads, LANES), F32),
        pltpu.VMEM((n_heads, LANES), F32),
        pltpu.VMEM((n_heads, HEAD_DIM, LANES), F32),
        pltpu.VMEM((width, LANES), F32),
    ]
    static = dict(n_steps=n_steps, pages=pages, n_heads=n_heads)
    return args, in_specs, out_spec, scratch, static


def _decode_attn(q_bf, k_new, v_new, cache_k, cache_v, page_table, lam_p, sub_g, *, layer, lam_init):
    bs, width = q_bf.shape
    args, in_specs, out_spec, scratch, static = _decode_operands(
        q_bf, k_new, v_new, cache_k, cache_v, page_table, lam_p, sub_g, layer=layer,
        seq_step=lambda b, t: (b, t))
    grid_spec = pltpu.PrefetchScalarGridSpec(
        num_scalar_prefetch=1, grid=(bs, 2 * static["n_steps"]),
        in_specs=in_specs, out_specs=out_spec, scratch_shapes=scratch)
    out = pl.pallas_call(
        functools.partial(_dec_attn_kernel, lam_init=lam_init, **static),
        out_shape=jax.ShapeDtypeStruct((bs, 1, width), BF16),
        grid_spec=grid_spec,
        compiler_params=_cparams(("parallel", "arbitrary")),
        name="decode_attn",
    )(page_table, *args)
    return out.reshape(bs, width)


def _diff_dec_kernel(pt_ref, lam_ref, g_ref, q_ref, kt_ref, vt_ref, *rest, tq, nq, n_dec, n_steps, pages,
                     n_heads, lam_init):
    del pt_ref
    dec_in = rest[0:n_dec]
    o_diff_ref, o_dec_ref = rest[n_dec:n_dec + 2]
    scratch = rest[n_dec + 2:]
    i = pl.program_id(2)
    step = (pl.program_id(0) * N_DIFF_HEADS + pl.program_id(1)) * nq + i
    _diff_body(i, lam_ref, g_ref, q_ref, kt_ref, vt_ref, o_diff_ref, tq, lam_init)
    _dec_body(step // (2 * n_steps), step % (2 * n_steps), dec_in, o_dec_ref, scratch,
              n_steps=n_steps, pages=pages, n_heads=n_heads, lam_init=lam_init)


def _fused_steps_match(batch, seq_len, tq, bs, n_pages):
    return batch * N_DIFF_HEADS * (seq_len // tq) == bs * 2 * (n_pages // PAGES_PER_STEP)


def _attn_diff_decode(q_p, ktb, vtb, q_s, k_new, v_new, cache_k, cache_v, page_table, lam_p, sub_g,
                      *, batch, seq_len, tq, layer, lam_init):
    m = q_p.shape[0]
    bs, width = q_s.shape
    nq = seq_len // tq
    off = N_SB_HEADS * HEAD_DIM // LANES
    dec_steps = 2 * (page_table.shape[1] // PAGES_PER_STEP)

    def seq_step(b, p, i):
        step = (b * N_DIFF_HEADS + p) * nq + i
        return step // dec_steps, step % dec_steps

    args, dec_specs, dec_out, scratch, static = _decode_operands(
        q_s, k_new, v_new, cache_k, cache_v, page_table, lam_p, sub_g, layer=layer, seq_step=seq_step)
    assert batch * N_DIFF_HEADS * nq == bs * dec_steps
    const = lambda shape: pl.BlockSpec(shape, lambda *ids: (0,) * len(shape))
    kv_spec = pl.BlockSpec((None, nq, LANES, tq), lambda b, p, i, pt: (b, 0, off + p, 0))
    grid_spec = pltpu.PrefetchScalarGridSpec(
        num_scalar_prefetch=1, grid=(batch, N_DIFF_HEADS, nq),
        in_specs=[const(lam_p.shape), const((1, LANES)),
                  pl.BlockSpec((tq, LANES), lambda b, p, i, pt: (b * nq + i, off + p)),
                  kv_spec, kv_spec] + dec_specs,
        out_specs=(pl.BlockSpec((tq, LANES), lambda b, p, i, pt: (b * nq + i, p)), dec_out),
        scratch_shapes=scratch)
    o_diff, o_dec = pl.pallas_call(
        functools.partial(_diff_dec_kernel, tq=tq, nq=nq, n_dec=len(args), lam_init=lam_init, **static),
        out_shape=(jax.ShapeDtypeStruct((m, N_DIFF_HEADS * LANES), BF16),
                   jax.ShapeDtypeStruct((bs, 1, width), BF16)),
        grid_spec=grid_spec,
        compiler_params=_cparams(("arbitrary", "arbitrary", "arbitrary")),
        name="attn_diff_decode",
    )(page_table, lam_p, sub_g.reshape(1, LANES), q_p, ktb, vtb, *args)
    return o_diff, o_dec.reshape(bs, width)


def _ffn_kernel(x_ref, g_ref, mod_ref, win_ref, cw_ref, cb_ref, wout_ref, fg_ref,
                y_ref, ff_ref, gs_ref, *, tiles_per_seq, tm, chunk, final):
    i = pl.program_id(0)
    seq = i // tiles_per_seq
    hidden = wout_ref.shape[0]
    halo = SUBLANES

    @pl.when(i % tiles_per_seq == 0)
    def _():
        gs_ref[0:halo, :] = jnp.zeros((halo, hidden), F32)

    x = x_ref[...]
    sh = _mod_row(mod_ref, 3, seq, False)
    sc = _mod_row(mod_ref, 4, seq, False)
    gate = _mod_row(mod_ref, 5, seq, False)
    h = _norm_mod(x, g_ref[...], sc, sh).astype(BF16)
    acc = jnp.zeros(x.shape, F32)
    n_chunks = hidden // chunk

    def up_proj(c):
        lo, up = c * chunk, (c + 1) * chunk
        return (jnp.dot(h, win_ref[:, lo:up], preferred_element_type=F32),
                jnp.dot(h, win_ref[:, hidden + lo:hidden + up], preferred_element_type=F32))

    nxt = up_proj(0)
    for c in range(n_chunks):
        lo, up = c * chunk, (c + 1) * chunk
        g, u = nxt
        if c + 1 < n_chunks:
            nxt = up_proj(c + 1)
        gs_ref[halo:halo + tm, lo:up] = g
        g1 = gs_ref[halo - 1:halo - 1 + tm, lo:up]
        g2 = gs_ref[halo - 2:halo - 2 + tm, lo:up]
        gc = cw_ref[0:1, lo:up] * g2 + cw_ref[1:2, lo:up] * g1 + cw_ref[2:3, lo:up] * g + cb_ref[:, lo:up]
        act = (_silu(gc) * u).astype(BF16)
        acc = acc + jnp.dot(act, wout_ref[lo:up, :], preferred_element_type=F32)
    ff_ref[...] = gs_ref[halo + tm - (FFN_CONV - 1):halo + tm, :]
    gs_ref[0:halo, :] = gs_ref[tm:tm + halo, :]
    y = x + gate * acc
    if final:
        ms = jnp.mean(y * y, axis=-1, keepdims=True)
        y = y * lax.rsqrt(ms + EPS) * fg_ref[...]
    y_ref[...] = y


def _ffn_prompt(x, norm_g, mod, win_bf, conv_w, conv_b, wout_bf, final_g, *, batch, seq_len, final):
    m, d = x.shape
    hidden = wout_bf.shape[0]
    tm = TM_FFN
    tps = seq_len // tm
    return pl.pallas_call(
        functools.partial(_ffn_kernel, tiles_per_seq=tps, tm=tm, chunk=FFN_CHUNK, final=final),
        out_shape=(jax.ShapeDtypeStruct((m, d), F32),
                   jax.ShapeDtypeStruct((batch, FFN_CONV - 1, hidden), F32)),
        grid=(m // tm,),
        in_specs=[
            pl.BlockSpec((tm, d), lambda i: (i, 0)),
            _resident((1, d)),
            _resident(mod.shape),
            _resident(win_bf.shape),
            _resident(conv_w.shape),
            _resident((1, hidden)),
            _resident(wout_bf.shape),
            _resident((1, d)),
        ],
        out_specs=(pl.BlockSpec((tm, d), lambda i: (i, 0)),
                   pl.BlockSpec((None, FFN_CONV - 1, hidden), lambda i: (i // tps, 0, 0))),
        scratch_shapes=[pltpu.VMEM((tm + SUBLANES, hidden), F32)],
        compiler_params=_cparams(("arbitrary",)),
        name="conv_ffn",
    )(x, norm_g.reshape(1, d), mod, win_bf, conv_w, conv_b.reshape(1, hidden), wout_bf,
      final_g.reshape(1, d))


def _ffn_dec_kernel(x_ref, g_ref, mod_ref, wg_ref, wu_ref, past_ref, cw_ref, cb_ref, wout_ref, fg_ref,
                    y_ref, gnew_ref, h_scr, acc_scr, *, final):
    c = pl.program_id(0)

    @pl.when(c == 0)
    def _():
        h = _norm_mod(x_ref[...], g_ref[...], mod_ref[4], mod_ref[3])
        h_scr[...] = h.astype(BF16)
        acc_scr[...] = jnp.zeros_like(acc_scr)

    h = h_scr[...]
    g = jnp.dot(h, wg_ref[...], preferred_element_type=F32)
    u = jnp.dot(h, wu_ref[...], preferred_element_type=F32)
    gnew_ref[...] = g
    gc = cw_ref[0:1, :] * past_ref[0] + cw_ref[1:2, :] * past_ref[1] + cw_ref[2:3, :] * g + cb_ref[...]
    act = (_silu(gc) * u).astype(BF16)
    acc_scr[...] = acc_scr[...] + jnp.dot(act, wout_ref[...], preferred_element_type=F32)

    @pl.when(c == pl.num_programs(0) - 1)
    def _():
        y = x_ref[...] + mod_ref[5] * acc_scr[...]
        if final:
            ms = jnp.mean(y * y, axis=-1, keepdims=True)
            y = y * lax.rsqrt(ms + EPS) * fg_ref[...]
        y_ref[...] = y


def _ffn_decode(x, norm_g, mod, win_bf, past_t, conv_w, conv_b, wout_bf, final_g, *, final):
    m, d = x.shape
    hidden = wout_bf.shape[0]
    chunk = FFN_CHUNK_DEC
    n = hidden // chunk
    return pl.pallas_call(
        functools.partial(_ffn_dec_kernel, final=final),
        out_shape=(jax.ShapeDtypeStruct((m, d), F32), jax.ShapeDtypeStruct((m, hidden), F32)),
        grid=(n,),
        in_specs=[
            pl.BlockSpec((m, d), lambda c: (0, 0)),
            pl.BlockSpec((1, d), lambda c: (0, 0)),
            pl.BlockSpec(mod.shape, lambda c: (0, 0, 0)),
            pl.BlockSpec((d, chunk), lambda c: (0, c)),
            pl.BlockSpec((d, chunk), lambda c: (0, n + c)),
            pl.BlockSpec((FFN_CONV - 1, m, chunk), lambda c: (0, 0, c)),
            pl.BlockSpec((FFN_CONV, chunk), lambda c: (0, c)),
            pl.BlockSpec((1, chunk), lambda c: (0, c)),
            pl.BlockSpec((chunk, d), lambda c: (c, 0)),
            pl.BlockSpec((1, d), lambda c: (0, 0)),
        ],
        out_specs=(pl.BlockSpec((m, d), lambda c: (0, 0)),
                   pl.BlockSpec((m, chunk), lambda c: (0, c))),
        scratch_shapes=[pltpu.VMEM((m, d), BF16), pltpu.VMEM((m, d), F32)],
        compiler_params=_cparams(("arbitrary",)),
        name="conv_ffn_decode",
    )(x, norm_g.reshape(1, d), mod, win_bf, win_bf, past_t, conv_w, conv_b.reshape(1, hidden),
      wout_bf, final_g.reshape(1, d))


def _mixer_kernel(x_ref, g_ref, mod_ref, win_ref, clg_ref, clb_ref, ws_ref, bs_ref, dw_ref, db_ref,
                  dlg_ref, dlb_ref, wout_ref, y_ref, cv_ref, cd_ref, a_scr, sh_scr,
                  *, tiles_per_seq, tm, width):
    i = pl.program_id(0)
    seq = i // tiles_per_seq
    halo = 4 * SUBLANES
    taps = D_CONV

    @pl.when(i % tiles_per_seq == 0)
    def _():
        a_scr[0:halo, :] = jnp.zeros((halo, width), F32)

    x = x_ref[...]
    sh = _mod_row(mod_ref, 0, seq, False)
    sc = _mod_row(mod_ref, 1, seq, False)
    gate = _mod_row(mod_ref, 2, seq, False)
    h = _norm_mod(x, g_ref[...], sc, sh).astype(BF16)

    zu, zv, za, zb = (jnp.dot(h, win_ref[:, n * width:(n + 1) * width], preferred_element_type=F32)
                      for n in range(4))

    u = _gelu_tanh(zu)
    v = _layer_norm(_gelu_tanh(zv), clg_ref[...], clb_ref[...])
    cv_ref[...] = v[tm - CHUNK:tm, :]
    vb = v.astype(BF16)
    gw = width // C_GROUPS
    r = lax.broadcasted_iota(jnp.int32, (CHUNK, CHUNK), 0)
    s = lax.broadcasted_iota(jnp.int32, (CHUNK, CHUNK), 1)
    mixed_rows = []
    for ch in range(tm // CHUNK):
        cols = []
        for grp in range(C_GROUPS):
            wm = jnp.where(s <= r, ws_ref[grp], 0.0).astype(BF16)
            vg = vb[ch * CHUNK:(ch + 1) * CHUNK, grp * gw:(grp + 1) * gw]
            cols.append(jnp.dot(wm, vg, preferred_element_type=F32))
        mixed_rows.append(jnp.concatenate(cols, axis=1) + bs_ref[...])
    s_out = u * jnp.concatenate(mixed_rows, axis=0)

    a = za * jax.nn.sigmoid(zb)
    a_scr[halo:halo + tm, :] = a
    base = halo - (taps - 1)
    dc = jnp.zeros((tm, width), F32) + db_ref[...]
    for s in range(SUBLANES):
        ks = list(range(s, taps, SUBLANES))
        rows = tm + SUBLANES * (len(ks) - 1)
        sh_scr[0:rows, :] = a_scr[base + s:base + s + rows, :]
        for n, k in enumerate(ks):
            dc = dc + dw_ref[k:k + 1, :] * sh_scr[SUBLANES * n:SUBLANES * n + tm, :]
    cd_ref[...] = a_scr[halo + tm - (taps - 1):halo + tm, :]
    a_scr[0:halo, :] = a_scr[tm:tm + halo, :]
    d_out = _silu(_layer_norm(dc, dlg_ref[...], dlb_ref[...]))

    o = jnp.dot(s_out.astype(BF16), wout_ref[0:width, :], preferred_element_type=F32)
    o = o + jnp.dot(d_out.astype(BF16), wout_ref[width:2 * width, :], preferred_element_type=F32)
    y_ref[...] = x + gate * o


def _mixer_prompt(x, norm_g, mod, win_bf, c_ln_g, c_ln_b, ws, bs, dw, db, d_ln_g, d_ln_b, wout_bf,
                  *, batch, seq_len):
    m, d = x.shape
    width = win_bf.shape[1] // 4
    tm = TM_MIX
    tps = seq_len // tm
    gw = width // C_GROUPS
    bs_exp = jnp.repeat(bs.T, gw, axis=1)
    row = lambda a: a.reshape(1, width)
    return pl.pallas_call(
        functools.partial(_mixer_kernel, tiles_per_seq=tps, tm=tm, width=width),
        out_shape=(jax.ShapeDtypeStruct((m, d), F32),
                   jax.ShapeDtypeStruct((batch, CHUNK, width), F32),
                   jax.ShapeDtypeStruct((batch, D_CONV - 1, width), F32)),
        grid=(m // tm,),
        in_specs=[
            pl.BlockSpec((tm, d), lambda i: (i, 0)),
            _resident((1, d)),
            _resident(mod.shape),
            _resident(win_bf.shape),
            _resident((1, width)), _resident((1, width)),
            _resident(ws.shape),
            _resident(bs_exp.shape),
            _resident(dw.shape),
            _resident((1, width)), _resident((1, width)), _resident((1, width)),
            _resident(wout_bf.shape),
        ],
        out_specs=(pl.BlockSpec((tm, d), lambda i: (i, 0)),
                   pl.BlockSpec((None, CHUNK, width), lambda i: (i // tps, 0, 0)),
                   pl.BlockSpec((None, D_CONV - 1, width), lambda i: (i // tps, 0, 0))),
        scratch_shapes=[pltpu.VMEM((tm + 4 * SUBLANES, width), F32),
                        pltpu.VMEM((tm + 3 * SUBLANES, width), F32)],
        compiler_params=_cparams(("arbitrary",)),
        name="mixer",
    )(x, norm_g.reshape(1, d), mod, win_bf, row(c_ln_g), row(c_ln_b), ws, bs_exp, dw, row(db),
      row(d_ln_g), row(d_ln_b), wout_bf)


def _mixer_dec_kernel(x_ref, g_ref, mod_ref, win_ref, clg_ref, clb_ref, w0_ref, b0_ref, past_ref,
                      dw_ref, db_ref, dlg_ref, dlb_ref, wout_ref, y_ref, v_ref, a_ref, *, width):
    x = x_ref[...]
    h = _norm_mod(x, g_ref[...], mod_ref[1], mod_ref[0]).astype(BF16)
    u = _gelu_tanh(jnp.dot(h, win_ref[:, 0:width], preferred_element_type=F32))
    v = _gelu_tanh(jnp.dot(h, win_ref[:, width:2 * width], preferred_element_type=F32))
    v = _layer_norm(v, clg_ref[...], clb_ref[...])
    v_ref[...] = v
    mixed = w0_ref[...] * v + b0_ref[...]
    s_out = u * mixed
    za = jnp.dot(h, win_ref[:, 2 * width:3 * width], preferred_element_type=F32)
    zb = jnp.dot(h, win_ref[:, 3 * width:4 * width], preferred_element_type=F32)
    a = za * jax.nn.sigmoid(zb)
    a_ref[...] = a
    dc = dw_ref[D_CONV - 1:D_CONV, :] * a + db_ref[...]
    for k in range(D_CONV - 1):
        dc = dc + dw_ref[k:k + 1, :] * past_ref[k]
    d_out = _silu(_layer_norm(dc, dlg_ref[...], dlb_ref[...]))
    o = jnp.dot(s_out.astype(BF16), wout_ref[0:width, :], preferred_element_type=F32)
    o = o + jnp.dot(d_out.astype(BF16), wout_ref[width:2 * width, :], preferred_element_type=F32)
    y_ref[...] = x + mod_ref[2] * o


def _mixer_decode(x, norm_g, mod, win_bf, c_ln_g, c_ln_b, ws, bs, past_t, dw, db, d_ln_g, d_ln_b, wout_bf):
    m, d = x.shape
    width = win_bf.shape[1] // 4
    gw = width // C_GROUPS
    row = lambda a: a.reshape(1, width)
    w0 = jnp.repeat(ws[:, 0, 0], gw).reshape(1, width)
    b0 = jnp.repeat(bs[:, 0], gw).reshape(1, width)
    args = (x, norm_g.reshape(1, d), mod, win_bf, row(c_ln_g), row(c_ln_b), w0, b0, past_t, dw, row(db),
            row(d_ln_g), row(d_ln_b), wout_bf)
    full = lambda a: pl.BlockSpec(a.shape, lambda i, nd=a.ndim: (0,) * nd)
    return pl.pallas_call(
        functools.partial(_mixer_dec_kernel, width=width),
        out_shape=(jax.ShapeDtypeStruct((m, d), F32),
                   jax.ShapeDtypeStruct((m, width), F32),
                   jax.ShapeDtypeStruct((m, width), F32)),
        grid=(1,),
        in_specs=[full(a) for a in args],
        out_specs=(pl.BlockSpec((m, d), lambda i: (0, 0)),
                   pl.BlockSpec((m, width), lambda i: (0, 0)),
                   pl.BlockSpec((m, width), lambda i: (0, 0))),
        compiler_params=_cparams(("arbitrary",)),
        name="mixer_decode",
    )(*args)


def kernel(x_prompt, x_sample, c_prompt, c_sample, cache_k, cache_v, state_conv_d, state_ffn_conv,
           page_table, norm_g, ada_w, ada_b, att_w_in, att_lam, att_sub_g, att_w_out, mix_w_in,
           c_ln_g, c_ln_b, c_ws, c_bs, d_conv_w, d_conv_b, d_ln_g, d_ln_b, mix_w_out,
           ffn_w_in, ffn_conv_w, ffn_conv_b, ffn_w_out, final_g):
    batch, seq_len, d = x_prompt.shape
    bs = x_sample.shape[0]
    depth = ada_w.shape[0]
    att_width = att_w_out.shape[1]
    n_heads = att_width // HEAD_DIM
    hidden = ffn_w_out.shape[1]
    assert x_sample.shape[1] == 1 and seq_len % TQ == 0 and seq_len % TM_PROJ == 0
    assert bs % (2 * SUBLANES) == 0 and page_table.shape[1] % PAGES_PER_STEP == 0

    rows = bs + batch
    pad = (-rows) % (2 * SUBLANES)
    c_all = jnp.concatenate([c_sample, c_prompt, jnp.zeros((pad, d), F32)], axis=0)
    mod = _ada_mod(c_all, ada_w, ada_b)
    mod_s = mod[:, :, 0:bs]
    mod_p = mod[:, :, bs:bs + batch]

    xp = x_prompt.reshape(batch * seq_len, d)
    xs = x_sample.reshape(bs, d)
    k_s, v_s, cv_p, cv_s, cd_p, cd_s, ff_p, ff_s = ([] for _ in range(8))
    kt_p = vt_p = None

    for i in range(depth):
        j = i // 2
        if i % 2 == 0:
            lam_init = 0.8 - 0.6 * math.exp(-0.3 * i)
            w_in = att_w_in[j].astype(BF16)
            w_out = att_w_out[j].astype(BF16)
            n_sb = N_SB_HEADS * HEAD_DIM
            wq = w_in[:, 0:att_width]
            wkt = w_in[:, att_width:2 * att_width].T
            wvt = w_in[:, 2 * att_width:3 * att_width].T
            q, kt_p, vt_p, ktb, vtb = _qkv_prompt(xp, norm_g[i, 0], mod_p[i], wq, wkt, wvt, kt_p, vt_p,
                                                  batch=batch, seq_len=seq_len, tm=TQ)
            q_s, k, v = _qkv_decode(xs, norm_g[i, 0], mod_s[i], w_in)
            o_sb = _attn_sb(q, ktb, vtb, batch=batch, seq_len=seq_len, tq=TQ)
            if _fused_steps_match(batch, seq_len, TQ, bs, page_table.shape[1]):
                o_d, o = _attn_diff_decode(q, ktb, vtb, q_s, k, v, cache_k, cache_v, page_table, att_lam[j],
                                           att_sub_g[j], batch=batch, seq_len=seq_len, tq=TQ, layer=j,
                                           lam_init=lam_init)
            else:
                o_d = _attn_diff(q, ktb, vtb, att_lam[j], att_sub_g[j], batch=batch, seq_len=seq_len, tq=TQ,
                                 lam_init=lam_init)
                o = _decode_attn(q_s, k, v, cache_k, cache_v, page_table, att_lam[j], att_sub_g[j],
                                 layer=j, lam_init=lam_init)
            xp = _out_proj(xp, o_sb, o_d, mod_p[i], w_out, tm=TM_PROJ,
                           tiles_per_seq=seq_len // TM_PROJ, per_row=False)
            xs = _out_proj(xs, o[:, 0:n_sb], o[:, n_sb:], mod_s[i], w_out, tm=bs, tiles_per_seq=1,
                           per_row=True)
            k_s.append(k.reshape(bs, 1, n_heads, HEAD_DIM))
            v_s.append(v.reshape(bs, 1, n_heads, HEAD_DIM))
        else:
            w_in = mix_w_in[j].astype(BF16)
            w_out = mix_w_out[j].astype(BF16)
            xp, cv, cd = _mixer_prompt(xp, norm_g[i, 0], mod_p[i], w_in, c_ln_g[j], c_ln_b[j], c_ws[j],
                                       c_bs[j], d_conv_w[j], d_conv_b[j], d_ln_g[j], d_ln_b[j], w_out,
                                       batch=batch, seq_len=seq_len)
            cv_p.append(cv)
            cd_p.append(cd)
            past = state_conv_d[j]
            xs, v_new, a_new = _mixer_decode(xs, norm_g[i, 0], mod_s[i], w_in, c_ln_g[j], c_ln_b[j],
                                             c_ws[j], c_bs[j], past.transpose(1, 0, 2), d_conv_w[j],
                                             d_conv_b[j], d_ln_g[j], d_ln_b[j], w_out)
            cv_s.append(v_new[:, None, :])
            cd_s.append(jnp.concatenate([past[:, 1:], a_new[:, None, :]], axis=1))
        final = i == depth - 1
        w_in = ffn_w_in[i].astype(BF16)
        w_out = ffn_w_out[i].astype(BF16)
        xp, ff = _ffn_prompt(xp, norm_g[i, 1], mod_p[i], w_in, ffn_conv_w[i], ffn_conv_b[i], w_out,
                             final_g, batch=batch, seq_len=seq_len, final=final)
        ff_p.append(ff)
        past = state_ffn_conv[i]
        xs, g_new = _ffn_decode(xs, norm_g[i, 1], mod_s[i], w_in, past.transpose(1, 0, 2), ffn_conv_w[i],
                                ffn_conv_b[i], w_out, final_g, final=final)
        ff_s.append(jnp.concatenate([past[:, 1:], g_new[:, None, :]], axis=1))

    def to_cache_layout(t):
        return t.reshape(t.shape[0], batch, n_heads, HEAD_DIM, seq_len).transpose(0, 1, 4, 2, 3)

    return (xp.reshape(batch, seq_len, d), xs.reshape(bs, 1, d),
            to_cache_layout(kt_p), to_cache_layout(vt_p), jnp.stack(k_s), jnp.stack(v_s),
            jnp.stack(cv_p), jnp.stack(cv_s), jnp.stack(cd_p), jnp.stack(cd_s),
            jnp.stack(ff_p), jnp.stack(ff_s))
```

```python
import functools
import math

import jax
import jax.numpy as jnp
from jax import lax
from jax.experimental import pallas as pl
from jax.experimental.pallas import tpu as pltpu

F32 = jnp.float32
BF16 = jnp.bfloat16

HEAD_DIM = 64
N_SB_HEADS = 8
N_DIFF_HEADS = 4
N_MOD = 6
C_GROUPS = 4
CHUNK = 128
D_CONV = 31
FFN_CONV = 3
EPS = 1e-6
NEG_INF = -1e30
SB_DEAD = 110.0

LANES = 128
SUBLANES = 8
VMEM_LIMIT = 56 * 1024 * 1024

TM_PROJ = 512
TM_FFN = 256
TM_MIX = 512
TQ = 256
PAGES_PER_STEP = 16
FFN_CHUNK = 256
FFN_CHUNK_DEC = 1408


def _cparams(sem, vmem=VMEM_LIMIT):
    return pltpu.CompilerParams(dimension_semantics=sem, vmem_limit_bytes=vmem)


def _resident(shape):
    nd = len(shape)
    return pl.BlockSpec(shape, lambda *_: (0,) * nd, pipeline_mode=pl.Buffered(1))


def _norm_mod(x, g, sc, sh):
    ms = jnp.mean(x * x, axis=-1, keepdims=True)
    y = x * lax.rsqrt(ms + EPS) * g
    return y * (1.0 + sc) + sh


def _mod_row(mod_ref, slot, seq, per_row):
    if per_row:
        return mod_ref[slot]
    return mod_ref[slot, pl.ds(seq, 1), :]


def _layer_norm(x, g, b):
    mu = jnp.mean(x, axis=-1, keepdims=True)
    xc = x - mu
    var = jnp.mean(xc * xc, axis=-1, keepdims=True)
    return xc * lax.rsqrt(var + EPS) * g + b


def _gelu_tanh(x):
    c = math.sqrt(2.0 / math.pi)
    return x * (0.5 * (1.0 + jnp.tanh(c * (x + 0.044715 * (x * x * x)))))


def _silu(x):
    return x * jax.nn.sigmoid(x)


def _softplus(z):
    return jnp.maximum(z, 0.0) + jnp.log(1.0 + jnp.exp(-jnp.abs(z)))


def _split_bf16(x):
    hi = x.astype(BF16)
    lo = (x - hi.astype(F32)).astype(BF16)
    return hi, lo


def _ada_kernel(c_ref, w_ref, b_ref, o_ref):
    c = c_ref[...]
    s = _silu(c).astype(BF16)
    w = w_ref[...].astype(BF16)
    o_ref[...] = jnp.dot(s, w, preferred_element_type=F32) + b_ref[...]


def _ada_mod(c_all, ada_w, ada_b):
    depth, d, _ = ada_w.shape
    rows = c_all.shape[0]
    return pl.pallas_call(
        _ada_kernel,
        out_shape=jax.ShapeDtypeStruct((depth, N_MOD, rows, d), F32),
        grid=(depth, N_MOD),
        in_specs=[
            pl.BlockSpec((rows, d), lambda i, j: (0, 0)),
            pl.BlockSpec((None, d, d), lambda i, j: (i, 0, j)),
            pl.BlockSpec((None, None, 1, d), lambda i, j: (i, j, 0, 0)),
        ],
        out_specs=pl.BlockSpec((None, None, rows, d), lambda i, j: (i, j, 0, 0)),
        compiler_params=_cparams(("parallel", "parallel")),
        name="ada_mod",
    )(c_all, ada_w, ada_b.reshape(depth, N_MOD, 1, d))


def _qkv_dec_kernel(x_ref, g_ref, mod_ref, w_ref, q_ref, k_ref, v_ref, *, width):
    h = _norm_mod(x_ref[...], g_ref[...], mod_ref[1], mod_ref[0]).astype(BF16)
    q = jnp.dot(h, w_ref[:, 0:width], preferred_element_type=F32)
    q_ref[...] = (q * (HEAD_DIM ** -0.5)).astype(BF16)
    k_ref[...] = jnp.dot(h, w_ref[:, width:2 * width], preferred_element_type=F32)
    v_ref[...] = jnp.dot(h, w_ref[:, 2 * width:3 * width], preferred_element_type=F32)


def _qkv_decode(x, norm_g, mod, w_bf):
    m, d = x.shape
    width = w_bf.shape[1] // 3
    tile = lambda dt: jax.ShapeDtypeStruct((m, width), dt)
    full = lambda shape: pl.BlockSpec(shape, lambda i: (0,) * len(shape))
    return pl.pallas_call(
        functools.partial(_qkv_dec_kernel, width=width),
        out_shape=(tile(BF16), tile(F32), tile(F32)),
        grid=(1,),
        in_specs=[full((m, d)), full((1, d)), full(mod.shape), full(w_bf.shape)],
        out_specs=(full((m, width)),) * 3,
        compiler_params=_cparams(("arbitrary",)),
        name="qkv_decode",
    )(x, norm_g.reshape(1, d), mod, w_bf)


def _qkv_t_kernel(x_ref, g_ref, mod_ref, wq_ref, wkt_ref, wvt_ref, *rest, tiles_per_seq, n_prev):
    if n_prev:
        pk_ref, pv_ref = rest[0:2]
        rest = rest[2:]
    q_ref, kt_ref, vt_ref, ktb_ref, vtb_ref = rest
    seq = pl.program_id(0) // tiles_per_seq
    sh = _mod_row(mod_ref, 0, seq, False)
    sc = _mod_row(mod_ref, 1, seq, False)
    h = _norm_mod(x_ref[...], g_ref[...], sc, sh).astype(BF16)
    q = jnp.dot(h, wq_ref[...], preferred_element_type=F32)
    q_ref[...] = (q * (HEAD_DIM ** -0.5)).astype(BF16)
    nt = (((1,), (1,)), ((), ()))
    kt = lax.dot_general(wkt_ref[...], h, nt, preferred_element_type=F32)
    vt = lax.dot_general(wvt_ref[...], h, nt, preferred_element_type=F32)
    if n_prev:
        kt_ref[0:n_prev] = pk_ref[...]
        vt_ref[0:n_prev] = pv_ref[...]
    kt_ref[n_prev] = kt
    vt_ref[n_prev] = vt
    ktb_ref[...] = kt.astype(BF16)
    vtb_ref[...] = vt.astype(BF16)


def _qkv_prompt(x, norm_g, mod, wq, wkt, wvt, prev_k, prev_v, *, batch, seq_len, tm):
    m, d = x.shape
    width = wq.shape[1]
    tps = seq_len // tm
    n_prev = 0 if prev_k is None else prev_k.shape[0]
    slab = lambda n: pl.BlockSpec((n, None, width, tm), lambda i: (0, i // tps, 0, i % tps))
    blocked = pl.BlockSpec((None, None, width, tm), lambda i: (i // tps, i % tps, 0, 0))
    in_specs = [
        pl.BlockSpec((tm, d), lambda i: (i, 0)),
        _resident((1, d)),
        _resident(mod.shape),
        _resident(wq.shape), _resident(wkt.shape), _resident(wvt.shape),
    ]
    args = [x, norm_g.reshape(1, d), mod, wq, wkt, wvt]
    if n_prev:
        in_specs += [slab(n_prev), slab(n_prev)]
        args += [prev_k, prev_v]
    stacked = jax.ShapeDtypeStruct((n_prev + 1, batch, width, seq_len), F32)
    blk = jax.ShapeDtypeStruct((batch, tps, width, tm), BF16)
    return pl.pallas_call(
        functools.partial(_qkv_t_kernel, tiles_per_seq=tps, n_prev=n_prev),
        out_shape=(jax.ShapeDtypeStruct((m, width), BF16), stacked, stacked, blk, blk),
        grid=(m // tm,),
        in_specs=in_specs,
        out_specs=(pl.BlockSpec((tm, width), lambda i: (i, 0)), slab(n_prev + 1), slab(n_prev + 1),
                   blocked, blocked),
        compiler_params=_cparams(("parallel",)),
        name="qkv_prompt",
    )(*args)


def _oproj_kernel(x_ref, oa_ref, ob_ref, mod_ref, w_ref, y_ref, *, tiles_per_seq, per_row, split):
    seq = pl.program_id(0) // tiles_per_seq
    gate = _mod_row(mod_ref, 2, seq, per_row)
    o = jnp.dot(oa_ref[...], w_ref[0:split, :], preferred_element_type=F32)
    o = o + jnp.dot(ob_ref[...], w_ref[split:, :], preferred_element_type=F32)
    y_ref[...] = x_ref[...] + gate * o


def _out_proj(x, oa, ob, mod, w_bf, *, tm, tiles_per_seq, per_row):
    m, d = x.shape
    split = oa.shape[1]
    return pl.pallas_call(
        functools.partial(_oproj_kernel, tiles_per_seq=tiles_per_seq, per_row=per_row, split=split),
        out_shape=jax.ShapeDtypeStruct((m, d), F32),
        grid=(m // tm,),
        in_specs=[
            pl.BlockSpec((tm, d), lambda i: (i, 0)),
            pl.BlockSpec((tm, split), lambda i: (i, 0)),
            pl.BlockSpec((tm, ob.shape[1]), lambda i: (i, 0)),
            _resident(mod.shape),
            _resident(w_bf.shape),
        ],
        out_specs=pl.BlockSpec((tm, d), lambda i: (i, 0)),
        compiler_params=_cparams(("parallel",)),
        name="out_proj",
    )(x, oa, ob, mod, w_bf)


_NT = (((1,), (1,)), ((), ()))


def _sb_kernel(q_ref, kt_ref, vt_ref, uu_ref, o_ref, *, tq):
    _sb_body(pl.program_id(2), q_ref, kt_ref, vt_ref, uu_ref, o_ref, tq)


def _sb_body(i, q_ref, kt_ref, vt_ref, uu_ref, o_ref, tq):
    q = q_ref[...]
    lane = lax.broadcasted_iota(jnp.int32, (1, LANES), 1)
    row = lax.broadcasted_iota(jnp.int32, (tq, tq), 0)
    col = lax.broadcasted_iota(jnp.int32, (tq, tq), 1)
    strict = col < row
    uu = uu_ref[...]
    zero_q = jnp.zeros_like(q)
    qs = (jnp.where(lane < HEAD_DIM, q, zero_q), jnp.where(lane >= HEAD_DIM, q, zero_q))

    def blocks(js, state, diag):
        kbs = [kt_ref[j] for j in js]
        vbs = [vt_ref[j] for j in js]
        chains = [(n, h) for n in range(len(js)) for h in range(2)]
        z = {c: jnp.dot(qs[c[1]], kbs[c[0]], preferred_element_type=F32) for c in chains}
        sp = {c: _softplus(z[c]) for c in chains}
        lk = {c: jnp.where(strict, sp[c], 0.0) if (diag and c[0] == 0) else sp[c] for c in chains}
        cs = {c: jnp.dot(jnp.concatenate(_split_bf16(lk[c]), axis=1), uu, preferred_element_type=F32)
              for c in chains}
        tot = {c: jnp.sum(lk[c], axis=1, keepdims=True) for c in chains}
        new = []
        for h in range(2):
            carry, acc = state[2 * h], state[2 * h + 1]
            for n in range(len(js)):
                c = (n, h)
                a = jnp.exp(z[c] - sp[c] - cs[c] - carry)
                if diag and n == 0:
                    a = jnp.where(strict, a, 0.0)
                acc = acc + lax.dot_general(a.astype(BF16), vbs[n], _NT, preferred_element_type=F32)
                carry = carry + tot[c]
            new += [carry, acc]
        return tuple(new)

    c0 = jnp.zeros((tq, 1), F32)
    a0 = jnp.zeros((tq, LANES), F32)
    state = (c0, a0, c0, a0)
    has_prev = jnp.minimum(i, 1)
    state = lax.fori_loop(0, 1 - has_prev, lambda t, st: blocks([i], st, True), state)
    state = lax.fori_loop(0, has_prev, lambda t, st: blocks([i, i - 1], st, True), state)
    rest = jnp.maximum(i - 1, 0)

    def live(st):
        return jnp.min(jnp.minimum(st[0], st[2])) < SB_DEAD

    def pair_cond(c):
        return jnp.logical_and(c[0] < rest // 2, live(c[1]))

    def pair_body(c):
        t, st = c
        return t + 1, blocks([i - 2 - 2 * t, i - 3 - 2 * t], st, False)

    _, state = lax.while_loop(pair_cond, pair_body, (jnp.int32(0), state))
    last = jnp.where(live(state), rest % 2, 0)
    state = lax.fori_loop(0, last, lambda t, st: blocks([0], st, False), state)
    o_ref[...] = jnp.where(lane < HEAD_DIM, state[1], state[3]).astype(o_ref.dtype)


def _suffix_matrix(n):
    j = lax.broadcasted_iota(jnp.int32, (n, n), 0)
    s = lax.broadcasted_iota(jnp.int32, (n, n), 1)
    u = (j > s).astype(BF16)
    return jnp.concatenate([u, u], axis=0)


def _attn_sb(q_bf, ktb, vtb, *, batch, seq_len, tq):
    m = q_bf.shape[0]
    nq = seq_len // tq
    n_pairs = N_SB_HEADS * HEAD_DIM // LANES
    kv_spec = pl.BlockSpec((None, nq, LANES, tq), lambda b, p, i: (b, 0, p, 0))
    return pl.pallas_call(
        functools.partial(_sb_kernel, tq=tq),
        out_shape=jax.ShapeDtypeStruct((m, n_pairs * LANES), BF16),
        grid=(batch, n_pairs, nq),
        in_specs=[
            pl.BlockSpec((tq, LANES), lambda b, p, i: (b * nq + i, p)),
            kv_spec, kv_spec,
            _resident((2 * tq, tq)),
        ],
        out_specs=pl.BlockSpec((tq, LANES), lambda b, p, i: (b * nq + i, p)),
        compiler_params=_cparams(("parallel", "parallel", "parallel")),
        name="attn_sb",
    )(q_bf, ktb, vtb, _suffix_matrix(tq))


def _lam_value(lam_ref, lam_init):
    lp = lam_ref[...]
    t1 = jnp.sum(lp[0:1] * lp[1:2], axis=1, keepdims=True)
    t2 = jnp.sum(lp[2:3] * lp[3:4], axis=1, keepdims=True)
    return jnp.exp(t1) - jnp.exp(t2) + lam_init


def _diff_kernel(lam_ref, g_ref, q_ref, kt_ref, vt_ref, o_ref, *, tq, lam_init):
    _diff_body(pl.program_id(2), lam_ref, g_ref, q_ref, kt_ref, vt_ref, o_ref, tq, lam_init)


def _diff_body(i, lam_ref, g_ref, q_ref, kt_ref, vt_ref, o_ref, tq, lam_init):
    q = q_ref[...]
    lane = lax.broadcasted_iota(jnp.int32, (1, LANES), 1)
    row = lax.broadcasted_iota(jnp.int32, (tq, 2 * tq), 0)
    col = lax.broadcasted_iota(jnp.int32, (tq, 2 * tq), 1)
    causal = jnp.logical_or(col <= row, col >= tq)
    zero_q = jnp.zeros_like(q)
    qs = (jnp.where(lane < HEAD_DIM, q, zero_q), jnp.where(lane >= HEAD_DIM, q, zero_q))

    def blocks(js, state, diag):
        kb = jnp.concatenate([kt_ref[j] for j in js], axis=1)
        vb = jnp.concatenate([vt_ref[j] for j in js], axis=1)
        s = [jnp.dot(qs[c], kb, preferred_element_type=F32) for c in range(2)]
        new = []
        for c in range(2):
            m_old, l, acc = state[3 * c:3 * c + 3]
            sc = jnp.where(causal[:, 0:len(js) * tq], s[c], NEG_INF) if diag else s[c]
            m_new = jnp.maximum(m_old, jnp.max(sc, axis=1, keepdims=True))
            alpha = jnp.exp(m_old - m_new)
            p = jnp.exp(sc - m_new)
            l = alpha * l + jnp.sum(p, axis=1, keepdims=True)
            acc = alpha * acc + lax.dot_general(p.astype(BF16), vb, _NT, preferred_element_type=F32)
            new += [m_new, l, acc]
        return tuple(new)

    m0 = jnp.full((tq, 1), NEG_INF, F32)
    l0 = jnp.zeros((tq, 1), F32)
    a0 = jnp.zeros((tq, LANES), F32)
    state = (m0, l0, a0, m0, l0, a0)
    has_prev = jnp.minimum(i, 1)
    state = lax.fori_loop(0, 1 - has_prev, lambda t, st: blocks([i], st, True), state)
    state = lax.fori_loop(0, has_prev, lambda t, st: blocks([i, i - 1], st, True), state)
    rest = jnp.maximum(i - 1, 0)
    state = lax.fori_loop(0, rest // 2, lambda t, st: blocks([i - 2 - 2 * t, i - 3 - 2 * t], st, False), state)
    state = lax.fori_loop(0, rest % 2, lambda t, st: blocks([0], st, False), state)

    lam = _lam_value(lam_ref, lam_init)
    o = state[2] / state[1] - lam * (state[5] / state[4])
    ms = jnp.mean(o * o, axis=1, keepdims=True)
    o = o * lax.rsqrt(ms + EPS) * g_ref[...] * (1.0 - lam_init)
    o_ref[...] = o.astype(o_ref.dtype)


def _attn_diff(q_bf, ktb, vtb, lam_p, sub_g, *, batch, seq_len, tq, lam_init):
    m = q_bf.shape[0]
    nq = seq_len // tq
    off = N_SB_HEADS * HEAD_DIM // LANES
    kv_spec = pl.BlockSpec((None, nq, LANES, tq), lambda b, p, i: (b, 0, off + p, 0))
    return pl.pallas_call(
        functools.partial(_diff_kernel, tq=tq, lam_init=lam_init),
        out_shape=jax.ShapeDtypeStruct((m, N_DIFF_HEADS * LANES), BF16),
        grid=(batch, N_DIFF_HEADS, nq),
        in_specs=[
            _resident(lam_p.shape),
            _resident((1, LANES)),
            pl.BlockSpec((tq, LANES), lambda b, p, i: (b * nq + i, off + p)),
            kv_spec, kv_spec,
        ],
        out_specs=pl.BlockSpec((tq, LANES), lambda b, p, i: (b * nq + i, p)),
        compiler_params=_cparams(("parallel", "parallel", "parallel")),
        name="attn_diff",
    )(lam_p, sub_g.reshape(1, LANES), q_bf, ktb, vtb)


def _dec_attn_kernel(pt_ref, *refs, n_steps, pages, n_heads, lam_init):
    del pt_ref
    _dec_body(pl.program_id(0), pl.program_id(1), refs[0:7 + 2 * pages], refs[7 + 2 * pages],
              refs[8 + 2 * pages:], n_steps=n_steps, pages=pages, n_heads=n_heads, lam_init=lam_init)


def _dec_body(seq, t, in_refs, o_ref, scratch, *, n_steps, pages, n_heads, lam_init):
    qt_ref, kx_ref, vx_ref, lam_ref, g_ref, uo_ref, msuf_ref = in_refs[0:7]
    k_refs = in_refs[7:7 + pages]
    v_refs = in_refs[7 + pages:7 + 2 * pages]
    s_scr, w_scr, x_scr, acc_scr, qb_scr = scratch
    n_pages = n_steps * pages
    n_rows = n_pages * n_heads
    lane = lax.broadcasted_iota(jnp.int32, (1, LANES), 1)
    rid = lax.broadcasted_iota(jnp.int32, (n_heads, 1), 0)
    is_sb = rid < N_SB_HEADS

    def value_row(wp, h):
        if h < N_SB_HEADS:
            return wp[h:h + 1, :]
        first = N_SB_HEADS + 2 * ((h - N_SB_HEADS) // 2)
        return wp[first:first + 1, :] + wp[first + 1:first + 2, :]

    def q_head(h):
        return qb_scr[h * HEAD_DIM:(h + 1) * HEAD_DIM, :]

    @pl.when(t == 0)
    def _():
        acc_scr[...] = jnp.zeros_like(acc_scr)
        src = lax.broadcasted_iota(jnp.int32, (LANES, LANES), 0)
        pick = jnp.where(src == seq, 1.0, 0.0).astype(BF16)
        qb_scr[...] = jnp.dot(qt_ref[...], pick, preferred_element_type=F32)

    @pl.when(t < n_steps)
    def _():
        for h in range(n_heads):
            qh = q_head(h)
            for r in range(pages):
                row = jnp.sum(k_refs[r][h] * qh, axis=0, keepdims=True)
                s_scr[t * pages + r, pl.ds(h, 1), :] = row

    @pl.when(t == n_steps - 1)
    def _():
        for h in range(n_heads):
            x_scr[pl.ds(h, 1), :] = jnp.sum(kx_ref[h] * q_head(h), axis=0, keepdims=True)
        s3 = s_scr[...]
        s = s3.reshape(n_rows, LANES)
        sp = _softplus(s)
        hi, lo = _split_bf16(sp)
        r1 = jnp.dot(jnp.concatenate([hi, lo], axis=1), uo_ref[...], preferred_element_type=F32)
        cs = r1[:, 0:LANES]
        tot = r1[:, LANES:2 * LANES]
        t1 = tot.astype(BF16)
        rem = tot - t1.astype(F32)
        t2, t3 = _split_bf16(rem)
        c3 = jnp.dot(msuf_ref[...], jnp.concatenate([t1, t2, t3], axis=1), preferred_element_type=F32)
        carry = c3[:, 0:LANES] + c3[:, LANES:2 * LANES] + c3[:, 2 * LANES:3 * LANES]
        a3 = jnp.exp(s - sp - cs - carry).reshape(n_pages, n_heads, LANES)
        sx = jnp.where(lane == seq, x_scr[...], NEG_INF)
        m = jnp.max(jnp.maximum(jnp.max(s3, axis=0), sx), axis=1, keepdims=True)
        p3 = jnp.exp(s3 - m[None])
        px = jnp.exp(sx - m)
        l = jnp.sum(jnp.sum(p3, axis=0) + px, axis=1, keepdims=True)
        lam = _lam_value(lam_ref, lam_init)
        coef = jnp.where(lax.bitwise_and(rid, 1) == 0, 1.0, -lam) / l
        w_scr[...] = jnp.where(is_sb[None], a3, p3 * coef[None])
        x_scr[...] = jnp.where(is_sb, 0.0, px * coef)

    @pl.when(t >= n_steps)
    def _():
        blk = t - n_steps
        wps = [w_scr[blk * pages + r] for r in range(pages)]
        for h in range(n_heads):
            acc = acc_scr[h]
            for r in range(pages):
                acc = acc + value_row(wps[r], h) * v_refs[r][h]
            acc_scr[h] = acc

    @pl.when(t == 2 * n_steps - 1)
    def _():
        wx = x_scr[...]
        for h in range(n_heads):
            acc_scr[h] = acc_scr[h] + value_row(wx, h) * vx_ref[h]
        hi, lo = _split_bf16(acc_scr[...].reshape(n_heads * HEAD_DIM, LANES))
        ones = jnp.ones((SUBLANES, LANES), BF16)
        nt = (((1,), (1,)), ((), ()))
        o = lax.dot_general(ones, hi, nt, preferred_element_type=F32)
        o = (o + lax.dot_general(ones, lo, nt, preferred_element_type=F32))[0:1, :]
        n_sb = N_SB_HEADS * HEAD_DIM
        o_ref[:, 0:n_sb] = o[:, 0:n_sb].astype(o_ref.dtype)
        for hd in range(N_DIFF_HEADS):
            od = o[:, n_sb + hd * LANES:n_sb + (hd + 1) * LANES]
            ms = jnp.mean(od * od, axis=1, keepdims=True)
            od = od * lax.rsqrt(ms + EPS) * g_ref[...] * (1.0 - lam_init)
            o_ref[:, n_sb + hd * LANES:n_sb + (hd + 1) * LANES] = od.astype(o_ref.dtype)


def _suffix_and_total_matrix():
    j = lax.broadcasted_iota(jnp.int32, (LANES, LANES), 0)
    s = lax.broadcasted_iota(jnp.int32, (LANES, LANES), 1)
    u1 = jnp.concatenate([(j > s).astype(BF16), jnp.ones((LANES, LANES), BF16)], axis=1)
    return jnp.concatenate([u1, u1], axis=0)


def _later_pages_matrix(n_pages, n_heads):
    n = n_pages * n_heads
    a = lax.broadcasted_iota(jnp.int32, (n, n), 0)
    b = lax.broadcasted_iota(jnp.int32, (n, n), 1)
    same_head = (a % n_heads) == (b % n_heads)
    return jnp.logical_and(same_head, b // n_heads > a // n_heads).astype(BF16)


def _decode_operands(q_bf, k_new, v_new, cache_k, cache_v, page_table, lam_p, sub_g, *, layer, seq_step):
    bs, width = q_bf.shape
    page, n_heads = cache_k.shape[2], cache_k.shape[3]
    assert page == LANES and cache_k.shape[4] == HEAD_DIM and bs <= LANES
    n_pages = page_table.shape[1]
    pages = PAGES_PER_STEP
    n_steps = n_pages // pages
    ck = cache_k.transpose(0, 1, 3, 4, 2)
    cv = cache_v.transpose(0, 1, 3, 4, 2)
    lanes = lambda a: jnp.pad(a.T, ((0, 0), (0, LANES - bs)))
    qt = lanes(q_bf)
    kx = lanes(k_new).reshape(n_heads, HEAD_DIM, LANES)
    vx = lanes(v_new).reshape(n_heads, HEAD_DIM, LANES)

    def k_map(r):
        def index(*ids):
            seq, t = seq_step(*ids[:-1])
            return (layer, ids[-1][seq, jnp.minimum(t, n_steps - 1) * pages + r], 0, 0, 0)
        return index

    def v_map(r):
        def index(*ids):
            seq, t = seq_step(*ids[:-1])
            return (layer, ids[-1][seq, jnp.maximum(t - n_steps, 0) * pages + r], 0, 0, 0)
        return index

    const = lambda shape: pl.BlockSpec(shape, lambda *ids: (0,) * len(shape))
    page_block = (None, None, n_heads, HEAD_DIM, LANES)
    n_rows = n_pages * n_heads
    in_specs = [const(qt.shape), const(kx.shape), const(vx.shape),
                const(lam_p.shape), const((1, LANES)), const((2 * LANES, 2 * LANES)),
                const((n_rows, n_rows))]
    in_specs += [pl.BlockSpec(page_block, k_map(r)) for r in range(pages)]
    in_specs += [pl.BlockSpec(page_block, v_map(r)) for r in range(pages)]
    args = [qt, kx, vx, lam_p, sub_g.reshape(1, LANES), _suffix_and_total_matrix(),
            _later_pages_matrix(n_pages, n_heads)] + [ck] * pages + [cv] * pages
    out_spec = pl.BlockSpec((None, 1, width), lambda *ids: (seq_step(*ids[:-1])[0], 0, 0))
    scratch = [
        pltpu.VMEM((n_pages, n_heads, LANES), F32),
        pltpu.VMEM((n_pages, n_heads, LANES), F32),
        pltpu.VMEM((n_heads, LANES), F32),
        pltpu.VMEM((n_heads, HEAD_DIM, LANES), F32),
        pltpu.VMEM((width, LANES), F32),
    ]
    static = dict(n_steps=n_steps, pages=pages, n_heads=n_heads)
    return args, in_specs, out_spec, scratch, static


def _decode_attn(q_bf, k_new, v_new, cache_k, cache_v, page_table, lam_p, sub_g, *, layer, lam_init):
    bs, width = q_bf.shape
    args, in_specs, out_spec, scratch, static = _decode_operands(
        q_bf, k_new, v_new, cache_k, cache_v, page_table, lam_p, sub_g, layer=layer,
        seq_step=lambda b, t: (b, t))
    grid_spec = pltpu.PrefetchScalarGridSpec(
        num_scalar_prefetch=1, grid=(bs, 2 * static["n_steps"]),
        in_specs=in_specs, out_specs=out_spec, scratch_shapes=scratch)
    out = pl.pallas_call(
        functools.partial(_dec_attn_kernel, lam_init=lam_init, **static),
        out_shape=jax.ShapeDtypeStruct((bs, 1, width), BF16),
        grid_spec=grid_spec,
        compiler_params=_cparams(("parallel", "arbitrary")),
        name="decode_attn",
    )(page_table, *args)
    return out.reshape(bs, width)


def _diff_dec_kernel(pt_ref, lam_ref, g_ref, q_ref, kt_ref, vt_ref, *rest, tq, nq, n_dec, n_steps, pages,
                     n_heads, lam_init):
    del pt_ref
    dec_in = rest[0:n_dec]
    o_diff_ref, o_dec_ref = rest[n_dec:n_dec + 2]
    scratch = rest[n_dec + 2:]
    i = pl.program_id(2)
    step = (pl.program_id(0) * N_DIFF_HEADS + pl.program_id(1)) * nq + i
    _diff_body(i, lam_ref, g_ref, q_ref, kt_ref, vt_ref, o_diff_ref, tq, lam_init)
    _dec_body(step // (2 * n_steps), step % (2 * n_steps), dec_in, o_dec_ref, scratch,
              n_steps=n_steps, pages=pages, n_heads=n_heads, lam_init=lam_init)


def _fused_steps_match(batch, seq_len, tq, bs, n_pages):
    return batch * N_DIFF_HEADS * (seq_len // tq) == bs * 2 * (n_pages // PAGES_PER_STEP)


def _attn_diff_decode(q_p, ktb, vtb, q_s, k_new, v_new, cache_k, cache_v, page_table, lam_p, sub_g,
                      *, batch, seq_len, tq, layer, lam_init):
    m = q_p.shape[0]
    bs, width = q_s.shape
    nq = seq_len // tq
    off = N_SB_HEADS * HEAD_DIM // LANES
    dec_steps = 2 * (page_table.shape[1] // PAGES_PER_STEP)

    def seq_step(b, p, i):
        step = (b * N_DIFF_HEADS + p) * nq + i
        return step // dec_steps, step % dec_steps

    args, dec_specs, dec_out, scratch, static = _decode_operands(
        q_s, k_new, v_new, cache_k, cache_v, page_table, lam_p, sub_g, layer=layer, seq_step=seq_step)
    assert batch * N_DIFF_HEADS * nq == bs * dec_steps
    const = lambda shape: pl.BlockSpec(shape, lambda *ids: (0,) * len(shape))
    kv_spec = pl.BlockSpec((None, nq, LANES, tq), lambda b, p, i, pt: (b, 0, off + p, 0))
    grid_spec = pltpu.PrefetchScalarGridSpec(
        num_scalar_prefetch=1, grid=(batch, N_DIFF_HEADS, nq),
        in_specs=[const(lam_p.shape), const((1, LANES)),
                  pl.BlockSpec((tq, LANES), lambda b, p, i, pt: (b * nq + i, off + p)),
                  kv_spec, kv_spec] + dec_specs,
        out_specs=(pl.BlockSpec((tq, LANES), lambda b, p, i, pt: (b * nq + i, p)), dec_out),
        scratch_shapes=scratch)
    o_diff, o_dec = pl.pallas_call(
        functools.partial(_diff_dec_kernel, tq=tq, nq=nq, n_dec=len(args), lam_init=lam_init, **static),
        out_shape=(jax.ShapeDtypeStruct((m, N_DIFF_HEADS * LANES), BF16),
                   jax.ShapeDtypeStruct((bs, 1, width), BF16)),
        grid_spec=grid_spec,
        compiler_params=_cparams(("arbitrary", "arbitrary", "arbitrary")),
        name="attn_diff_decode",
    )(page_table, lam_p, sub_g.reshape(1, LANES), q_p, ktb, vtb, *args)
    return o_diff, o_dec.reshape(bs, width)


def _ffn_kernel(x_ref, g_ref, mod_ref, win_ref, cw_ref, cb_ref, wout_ref, fg_ref,
                y_ref, ff_ref, gs_ref, *, tiles_per_seq, tm, chunk, final):
    i = pl.program_id(0)
    seq = i // tiles_per_seq
    hidden = wout_ref.shape[0]
    halo = SUBLANES

    @pl.when(i % tiles_per_seq == 0)
    def _():
        gs_ref[0:halo, :] = jnp.zeros((halo, hidden), F32)

    x = x_ref[...]
    sh = _mod_row(mod_ref, 3, seq, False)
    sc = _mod_row(mod_ref, 4, seq, False)
    gate = _mod_row(mod_ref, 5, seq, False)
    h = _norm_mod(x, g_ref[...], sc, sh).astype(BF16)
    acc = jnp.zeros(x.shape, F32)
    n_chunks = hidden // chunk

    def up_proj(c):
        lo, up = c * chunk, (c + 1) * chunk
        return (jnp.dot(h, win_ref[:, lo:up], preferred_element_type=F32),
                jnp.dot(h, win_ref[:, hidden + lo:hidden + up], preferred_element_type=F32))

    nxt = up_proj(0)
    for c in range(n_chunks):
        lo, up = c * chunk, (c + 1) * chunk
        g, u = nxt
        if c + 1 < n_chunks:
            nxt = up_proj(c + 1)
        gs_ref[halo:halo + tm, lo:up] = g
        g1 = gs_ref[halo - 1:halo - 1 + tm, lo:up]
        g2 = gs_ref[halo - 2:halo - 2 + tm, lo:up]
        gc = cw_ref[0:1, lo:up] * g2 + cw_ref[1:2, lo:up] * g1 + cw_ref[2:3, lo:up] * g + cb_ref[:, lo:up]
        act = (_silu(gc) * u).astype(BF16)
        acc = acc + jnp.dot(act, wout_ref[lo:up, :], preferred_element_type=F32)
    ff_ref[...] = gs_ref[halo + tm - (FFN_CONV - 1):halo + tm, :]
    gs_ref[0:halo, :] = gs_ref[tm:tm + halo, :]
    y = x + gate * acc
    if final:
        ms = jnp.mean(y * y, axis=-1, keepdims=True)
        y = y * lax.rsqrt(ms + EPS) * fg_ref[...]
    y_ref[...] = y


def _ffn_prompt(x, norm_g, mod, win_bf, conv_w, conv_b, wout_bf, final_g, *, batch, seq_len, final):
    m, d = x.shape
    hidden = wout_bf.shape[0]
    tm = TM_FFN
    tps = seq_len // tm
    return pl.pallas_call(
        functools.partial(_ffn_kernel, tiles_per_seq=tps, tm=tm, chunk=FFN_CHUNK, final=final),
        out_shape=(jax.ShapeDtypeStruct((m, d), F32),
                   jax.ShapeDtypeStruct((batch, FFN_CONV - 1, hidden), F32)),
        grid=(m // tm,),
        in_specs=[
            pl.BlockSpec((tm, d), lambda i: (i, 0)),
            _resident((1, d)),
            _resident(mod.shape),
            _resident(win_bf.shape),
            _resident(conv_w.shape),
            _resident((1, hidden)),
            _resident(wout_bf.shape),
            _resident((1, d)),
        ],
        out_specs=(pl.BlockSpec((tm, d), lambda i: (i, 0)),
                   pl.BlockSpec((None, FFN_CONV - 1, hidden), lambda i: (i // tps, 0, 0))),
        scratch_shapes=[pltpu.VMEM((tm + SUBLANES, hidden), F32)],
        compiler_params=_cparams(("arbitrary",)),
        name="conv_ffn",
    )(x, norm_g.reshape(1, d), mod, win_bf, conv_w, conv_b.reshape(1, hidden), wout_bf,
      final_g.reshape(1, d))


def _ffn_dec_kernel(x_ref, g_ref, mod_ref, wg_ref, wu_ref, past_ref, cw_ref, cb_ref, wout_ref, fg_ref,
                    y_ref, gnew_ref, h_scr, acc_scr, *, final):
    c = pl.program_id(0)

    @pl.when(c == 0)
    def _():
        h = _norm_mod(x_ref[...], g_ref[...], mod_ref[4], mod_ref[3])
        h_scr[...] = h.astype(BF16)
        acc_scr[...] = jnp.zeros_like(acc_scr)

    h = h_scr[...]
    g = jnp.dot(h, wg_ref[...], preferred_element_type=F32)
    u = jnp.dot(h, wu_ref[...], preferred_element_type=F32)
    gnew_ref[...] = g
    gc = cw_ref[0:1, :] * past_ref[0] + cw_ref[1:2, :] * past_ref[1] + cw_ref[2:3, :] * g + cb_ref[...]
    act = (_silu(gc) * u).astype(BF16)
    acc_scr[...] = acc_scr[...] + jnp.dot(act, wout_ref[...], preferred_element_type=F32)

    @pl.when(c == pl.num_programs(0) - 1)
    def _():
        y = x_ref[...] + mod_ref[5] * acc_scr[...]
        if final:
            ms = jnp.mean(y * y, axis=-1, keepdims=True)
            y = y * lax.rsqrt(ms + EPS) * fg_ref[...]
        y_ref[...] = y


def _ffn_decode(x, norm_g, mod, win_bf, past_t, conv_w, conv_b, wout_bf, final_g, *, final):
    m, d = x.shape
    hidden = wout_bf.shape[0]
    chunk = FFN_CHUNK_DEC
    n = hidden // chunk
    return pl.pallas_call(
        functools.partial(_ffn_dec_kernel, final=final),
        out_shape=(jax.ShapeDtypeStruct((m, d), F32), jax.ShapeDtypeStruct((m, hidden), F32)),
        grid=(n,),
        in_specs=[
            pl.BlockSpec((m, d), lambda c: (0, 0)),
            pl.BlockSpec((1, d), lambda c: (0, 0)),
            pl.BlockSpec(mod.shape, lambda c: (0, 0, 0)),
            pl.BlockSpec((d, chunk), lambda c: (0, c)),
            pl.BlockSpec((d, chunk), lambda c: (0, n + c)),
            pl.BlockSpec((FFN_CONV - 1, m, chunk), lambda c: (0, 0, c)),
            pl.BlockSpec((FFN_CONV, chunk), lambda c: (0, c)),
            pl.BlockSpec((1, chunk), lambda c: (0, c)),
            pl.BlockSpec((chunk, d), lambda c: (c, 0)),
            pl.BlockSpec((1, d), lambda c: (0, 0)),
        ],
        out_specs=(pl.BlockSpec((m, d), lambda c: (0, 0)),
                   pl.BlockSpec((m, chunk), lambda c: (0, c))),
        scratch_shapes=[pltpu.VMEM((m, d), BF16), pltpu.VMEM((m, d), F32)],
        compiler_params=_cparams(("arbitrary",)),
        name="conv_ffn_decode",
    )(x, norm_g.reshape(1, d), mod, win_bf, win_bf, past_t, conv_w, conv_b.reshape(1, hidden),
      wout_bf, final_g.reshape(1, d))


def _mixer_kernel(x_ref, g_ref, mod_ref, win_ref, clg_ref, clb_ref, ws_ref, bs_ref, dw_ref, db_ref,
                  dlg_ref, dlb_ref, wout_ref, y_ref, cv_ref, cd_ref, a_scr, sh_scr,
                  *, tiles_per_seq, tm, width):
    i = pl.program_id(0)
    seq = i // tiles_per_seq
    halo = 4 * SUBLANES
    taps = D_CONV

    @pl.when(i % tiles_per_seq == 0)
    def _():
        a_scr[0:halo, :] = jnp.zeros((halo, width), F32)

    x = x_ref[...]
    sh = _mod_row(mod_ref, 0, seq, False)
    sc = _mod_row(mod_ref, 1, seq, False)
    gate = _mod_row(mod_ref, 2, seq, False)
    h = _norm_mod(x, g_ref[...], sc, sh).astype(BF16)

    zu, zv, za, zb = (jnp.dot(h, win_ref[:, n * width:(n + 1) * width], preferred_element_type=F32)
                      for n in range(4))

    u = _gelu_tanh(zu)
    v = _layer_norm(_gelu_tanh(zv), clg_ref[...], clb_ref[...])
    cv_ref[...] = v[tm - CHUNK:tm, :]
    vb = v.astype(BF16)
    gw = width // C_GROUPS
    r = lax.broadcasted_iota(jnp.int32, (CHUNK, CHUNK), 0)
    s = lax.broadcasted_iota(jnp.int32, (CHUNK, CHUNK), 1)
    mixed_rows = []
    for ch in range(tm // CHUNK):
        cols = []
        for grp in range(C_GROUPS):
            wm = jnp.where(s <= r, ws_ref[grp], 0.0).astype(BF16)
            vg = vb[ch * CHUNK:(ch + 1) * CHUNK, grp * gw:(grp + 1) * gw]
            cols.append(jnp.dot(wm, vg, preferred_element_type=F32))
        mixed_rows.append(jnp.concatenate(cols, axis=1) + bs_ref[...])
    s_out = u * jnp.concatenate(mixed_rows, axis=0)

    a = za * jax.nn.sigmoid(zb)
    a_scr[halo:halo + tm, :] = a
    base = halo - (taps - 1)
    dc = jnp.zeros((tm, width), F32) + db_ref[...]
    for s in range(SUBLANES):
        ks = list(range(s, taps, SUBLANES))
        rows = tm + SUBLANES * (len(ks) - 1)
        sh_scr[0:rows, :] = a_scr[base + s:base + s + rows, :]
        for n, k in enumerate(ks):
            dc = dc + dw_ref[k:k + 1, :] * sh_scr[SUBLANES * n:SUBLANES * n + tm, :]
    cd_ref[...] = a_scr[halo + tm - (taps - 1):halo + tm, :]
    a_scr[0:halo, :] = a_scr[tm:tm + halo, :]
    d_out = _silu(_layer_norm(dc, dlg_ref[...], dlb_ref[...]))

    o = jnp.dot(s_out.astype(BF16), wout_ref[0:width, :], preferred_element_type=F32)
    o = o + jnp.dot(d_out.astype(BF16), wout_ref[width:2 * width, :], preferred_element_type=F32)
    y_ref[...] = x + gate * o


def _mixer_prompt(x, norm_g, mod, win_bf, c_ln_g, c_ln_b, ws, bs, dw, db, d_ln_g, d_ln_b, wout_bf,
                  *, batch, seq_len):
    m, d = x.shape
    width = win_bf.shape[1] // 4
    tm = TM_MIX
    tps = seq_len // tm
    gw = width // C_GROUPS
    bs_exp = jnp.repeat(bs.T, gw, axis=1)
    row = lambda a: a.reshape(1, width)
    return pl.pallas_call(
        functools.partial(_mixer_kernel, tiles_per_seq=tps, tm=tm, width=width),
        out_shape=(jax.ShapeDtypeStruct((m, d), F32),
                   jax.ShapeDtypeStruct((batch, CHUNK, width), F32),
                   jax.ShapeDtypeStruct((batch, D_CONV - 1, width), F32)),
        grid=(m // tm,),
        in_specs=[
            pl.BlockSpec((tm, d), lambda i: (i, 0)),
            _resident((1, d)),
            _resident(mod.shape),
            _resident(win_bf.shape),
            _resident((1, width)), _resident((1, width)),
            _resident(ws.shape),
            _resident(bs_exp.shape),
            _resident(dw.shape),
            _resident((1, width)), _resident((1, width)), _resident((1, width)),
            _resident(wout_bf.shape),
        ],
        out_specs=(pl.BlockSpec((tm, d), lambda i: (i, 0)),
                   pl.BlockSpec((None, CHUNK, width), lambda i: (i // tps, 0, 0)),
                   pl.BlockSpec((None, D_CONV - 1, width), lambda i: (i // tps, 0, 0))),
        scratch_shapes=[pltpu.VMEM((tm + 4 * SUBLANES, width), F32),
                        pltpu.VMEM((tm + 3 * SUBLANES, width), F32)],
        compiler_params=_cparams(("arbitrary",)),
        name="mixer",
    )(x, norm_g.reshape(1, d), mod, win_bf, row(c_ln_g), row(c_ln_b), ws, bs_exp, dw, row(db),
      row(d_ln_g), row(d_ln_b), wout_bf)


def _mixer_dec_kernel(x_ref, g_ref, mod_ref, win_ref, clg_ref, clb_ref, w0_ref, b0_ref, past_ref,
                      dw_ref, db_ref, dlg_ref, dlb_ref, wout_ref, y_ref, v_ref, a_ref, *, width):
    x = x_ref[...]
    h = _norm_mod(x, g_ref[...], mod_ref[1], mod_ref[0]).astype(BF16)
    u = _gelu_tanh(jnp.dot(h, win_ref[:, 0:width], preferred_element_type=F32))
    v = _gelu_tanh(jnp.dot(h, win_ref[:, width:2 * width], preferred_element_type=F32))
    v = _layer_norm(v, clg_ref[...], clb_ref[...])
    v_ref[...] = v
    mixed = w0_ref[...] * v + b0_ref[...]
    s_out = u * mixed
    za = jnp.dot(h, win_ref[:, 2 * width:3 * width], preferred_element_type=F32)
    zb = jnp.dot(h, win_ref[:, 3 * width:4 * width], preferred_element_type=F32)
    a = za * jax.nn.sigmoid(zb)
    a_ref[...] = a
    dc = dw_ref[D_CONV - 1:D_CONV, :] * a + db_ref[...]
    for k in range(D_CONV - 1):
        dc = dc + dw_ref[k:k + 1, :] * past_ref[k]
    d_out = _silu(_layer_norm(dc, dlg_ref[...], dlb_ref[...]))
    o = jnp.dot(s_out.astype(BF16), wout_ref[0:width, :], preferred_element_type=F32)
    o = o + jnp.dot(d_out.astype(BF16), wout_ref[width:2 * width, :], preferred_element_type=F32)
    y_ref[...] = x + mod_ref[2] * o


def _mixer_decode(x, norm_g, mod, win_bf, c_ln_g, c_ln_b, ws, bs, past_t, dw, db, d_ln_g, d_ln_b, wout_bf):
    m, d = x.shape
    width = win_bf.shape[1] // 4
    gw = width // C_GROUPS
    row = lambda a: a.reshape(1, width)
    w0 = jnp.repeat(ws[:, 0, 0], gw).reshape(1, width)
    b0 = jnp.repeat(bs[:, 0], gw).reshape(1, width)
    args = (x, norm_g.reshape(1, d), mod, win_bf, row(c_ln_g), row(c_ln_b), w0, b0, past_t, dw, row(db),
            row(d_ln_g), row(d_ln_b), wout_bf)
    full = lambda a: pl.BlockSpec(a.shape, lambda i, nd=a.ndim: (0,) * nd)
    return pl.pallas_call(
        functools.partial(_mixer_dec_kernel, width=width),
        out_shape=(jax.ShapeDtypeStruct((m, d), F32),
                   jax.ShapeDtypeStruct((m, width), F32),
                   jax.ShapeDtypeStruct((m, width), F32)),
        grid=(1,),
        in_specs=[full(a) for a in args],
        out_specs=(pl.BlockSpec((m, d), lambda i: (0, 0)),
                   pl.BlockSpec((m, width), lambda i: (0, 0)),
                   pl.BlockSpec((m, width), lambda i: (0, 0))),
        compiler_params=_cparams(("arbitrary",)),
        name="mixer_decode",
    )(*args)


def kernel(x_prompt, x_sample, c_prompt, c_sample, cache_k, cache_v, state_conv_d, state_ffn_conv,
           page_table, norm_g, ada_w, ada_b, att_w_in, att_lam, att_sub_g, att_w_out, mix_w_in,
           c_ln_g, c_ln_b, c_ws, c_bs, d_conv_w, d_conv_b, d_ln_g, d_ln_b, mix_w_out,
           ffn_w_in, ffn_conv_w, ffn_conv_b, ffn_w_out, final_g):
    batch, seq_len, d = x_prompt.shape
    bs = x_sample.shape[0]
    depth = ada_w.shape[0]
    att_width = att_w_out.shape[1]
    n_heads = att_width // HEAD_DIM
    hidden = ffn_w_out.shape[1]
    assert x_sample.shape[1] == 1 and seq_len % TQ == 0 and seq_len % TM_PROJ == 0
    assert bs % (2 * SUBLANES) == 0 and page_table.shape[1] % PAGES_PER_STEP == 0

    rows = bs + batch
    pad = (-rows) % (2 * SUBLANES)
    c_all = jnp.concatenate([c_sample, c_prompt, jnp.zeros((pad, d), F32)], axis=0)
    mod = _ada_mod(c_all, ada_w, ada_b)
    mod_s = mod[:, :, 0:bs]
    mod_p = mod[:, :, bs:bs + batch]

    xp = x_prompt.reshape(batch * seq_len, d)
    xs = x_sample.reshape(bs, d)
    k_s, v_s, cv_p, cv_s, cd_p, cd_s, ff_p, ff_s = ([] for _ in range(8))
    kt_p = vt_p = None

    for i in range(depth):
        j = i // 2
        if i % 2 == 0:
            lam_init = 0.8 - 0.6 * math.exp(-0.3 * i)
            w_in = att_w_in[j].astype(BF16)
            w_out = att_w_out[j].astype(BF16)
            n_sb = N_SB_HEADS * HEAD_DIM
            wq = w_in[:, 0:att_width]
            wkt = w_in[:, att_width:2 * att_width].T
            wvt = w_in[:, 2 * att_width:3 * att_width].T
            q, kt_p, vt_p, ktb, vtb = _qkv_prompt(xp, norm_g[i, 0], mod_p[i], wq, wkt, wvt, kt_p, vt_p,
                                                  batch=batch, seq_len=seq_len, tm=TQ)
            q_s, k, v = _qkv_decode(xs, norm_g[i, 0], mod_s[i], w_in)
            o_sb = _attn_sb(q, ktb, vtb, batch=batch, seq_len=seq_len, tq=TQ)
            if _fused_steps_match(batch, seq_len, TQ, bs, page_table.shape[1]):
                o_d, o = _attn_diff_decode(q, ktb, vtb, q_s, k, v, cache_k, cache_v, page_table, att_lam[j],
                                           att_sub_g[j], batch=batch, seq_len=seq_len, tq=TQ, layer=j,
                                           lam_init=lam_init)
            else:
                o_d = _attn_diff(q, ktb, vtb, att_lam[j], att_sub_g[j], batch=batch, seq_len=seq_len, tq=TQ,
                                 lam_init=lam_init)
                o = _decode_attn(q_s, k, v, cache_k, cache_v, page_table, att_lam[j], att_sub_g[j],
                                 layer=j, lam_init=lam_init)
            xp = _out_proj(xp, o_sb, o_d, mod_p[i], w_out, tm=TM_PROJ,
                           tiles_per_seq=seq_len // TM_PROJ, per_row=False)
            xs = _out_proj(xs, o[:, 0:n_sb], o[:, n_sb:], mod_s[i], w_out, tm=bs, tiles_per_seq=1,
                           per_row=True)
            k_s.append(k.reshape(bs, 1, n_heads, HEAD_DIM))
            v_s.append(v.reshape(bs, 1, n_heads, HEAD_DIM))
        else:
            w_in = mix_w_in[j].astype(BF16)
            w_out = mix_w_out[j].astype(BF16)
            xp, cv, cd = _mixer_prompt(xp, norm_g[i, 0], mod_p[i], w_in, c_ln_g[j], c_ln_b[j], c_ws[j],
                                       c_bs[j], d_conv_w[j], d_conv_b[j], d_ln_g[j], d_ln_b[j], w_out,
                                       batch=batch, seq_len=seq_len)
            cv_p.append(cv)
            cd_p.append(cd)
            past = state_conv_d[j]
            xs, v_new, a_new = _mixer_decode(xs, norm_g[i, 0], mod_s[i], w_in, c_ln_g[j], c_ln_b[j],
                                             c_ws[j], c_bs[j], past.transpose(1, 0, 2), d_conv_w[j],
                                             d_conv_b[j], d_ln_g[j], d_ln_b[j], w_out)
            cv_s.append(v_new[:, None, :])
            cd_s.append(jnp.concatenate([past[:, 1:], a_new[:, None, :]], axis=1))
        final = i == depth - 1
        w_in = ffn_w_in[i].astype(BF16)
        w_out = ffn_w_out[i].astype(BF16)
        xp, ff = _ffn_prompt(xp, norm_g[i, 1], mod_p[i], w_in, ffn_conv_w[i], ffn_conv_b[i], w_out,
                             final_g, batch=batch, seq_len=seq_len, final=final)
        ff_p.append(ff)
        past = state_ffn_conv[i]
        xs, g_new = _ffn_decode(xs, norm_g[i, 1], mod_s[i], w_in, past.transpose(1, 0, 2), ffn_conv_w[i],
                                ffn_conv_b[i], w_out, final_g, final=final)
        ff_s.append(jnp.concatenate([past[:, 1:], g_new[:, None, :]], axis=1))

    def to_cache_layout(t):
        return t.reshape(t.shape[0], batch, n_heads, HEAD_DIM, seq_len).transpose(0, 1, 4, 2, 3)

    return (xp.reshape(batch, seq_len, d), xs.reshape(bs, 1, d),
            to_cache_layout(kt_p), to_cache_layout(vt_p), jnp.stack(k_s), jnp.stack(v_s),
            jnp.stack(cv_p), jnp.stack(cv_s), jnp.stack(cd_p), jnp.stack(cd_s),
            jnp.stack(ff_p), jnp.stack(ff_s))
```
